```python
import jax, jax.numpy as jnp
from jax import lax
import numpy as np

D_MODEL = 2048
BATCH = 2
SEQ = 4096
DEPTH = 1

N_META = 16
CONV_DIM = D_MODEL // 2
CONV_GROUPS = 16
CONV_K = 3
ATTN_HEADS = 16
HEAD_DIM = (D_MODEL - CONV_DIM) // ATTN_HEADS
ATTN_DIM = ATTN_HEADS * HEAD_DIM
PROJ_DIM = 3 * CONV_DIM + 3 * ATTN_DIM
D_FF = 4 * D_MODEL
GRID_W = 64
WIN_ROWS = 8
WIN_COLS = 16
COL_BLOCK = WIN_COLS
KEY_COL_BLOCK = 2 * WIN_COLS
RMS_EPS = 1e-6
NEG_INF = -1e30

kernel_name = "hybrid_shortconv_natten2d_block"


def rms_norm(x, g):
    xf = x.astype(jnp.float32)
    y = xf * lax.rsqrt(jnp.mean(xf * xf, axis=-1, keepdims=True) + RMS_EPS)
    return (y * g.astype(jnp.float32)).astype(x.dtype)


def short_conv_mixer(b_gate, c_gate, u, conv_w, conv_b):
    v = c_gate * u
    kern = conv_w.astype(v.dtype)[:, None, :]
    y = lax.conv_general_dilated(
        v, kern, window_strides=(1,), padding=[(CONV_K // 2, CONV_K // 2)],
        dimension_numbers=("NWC", "WIO", "NWC"), feature_group_count=v.shape[-1])
    return b_gate * (y + conv_b.astype(v.dtype))


def neighbourhood_attention(q, k, v, rpb):
    B, L, H, dh = q.shape
    S = L - N_META
    rows = S // GRID_W
    kh = min(WIN_ROWS, rows)
    nqb = GRID_W // COL_BLOCK
    scale = dh ** -0.5

    qm, km, vm = q[:, :N_META], k[:, :N_META], v[:, :N_META]
    qg = q[:, N_META:].reshape(B, rows, nqb, COL_BLOCK, H, dh)
    kg = k[:, N_META:].reshape(B, rows, GRID_W, H, dh)
    vg = v[:, N_META:].reshape(B, rows, GRID_W, H, dh)

    r = np.arange(rows)
    row_start = np.clip(r - kh // 2, 0, rows - kh)
    row_idx = row_start[:, None] + np.arange(kh)
    jb = np.arange(nqb)
    blk_start = np.clip(jb * COL_BLOCK - WIN_COLS // 2, 0, GRID_W - KEY_COL_BLOCK)
    col_idx = blk_start[:, None] + np.arange(KEY_COL_BLOCK)
    qc = jb[:, None] * COL_BLOCK + np.arange(COL_BLOCK)
    c0 = np.clip(qc - WIN_COLS // 2, 0, GRID_W - WIN_COLS)
    col_valid = (col_idx[:, None, :] >= c0[:, :, None]) & (col_idx[:, None, :] < c0[:, :, None] + WIN_COLS)
    dr_idx = row_idx - r[:, None] + (WIN_ROWS - 1)
    dc_idx = np.clip(col_idx[:, None, :] - qc[:, :, None] + (WIN_COLS - 1), 0, 2 * WIN_COLS - 2)

    gr = row_idx[:, :, None, None]
    gc = col_idx[None, None, :, :]
    k_win = kg[:, gr, gc]
    v_win = vg[:, gr, gc]

    bias = rpb[:, dr_idx[:, None, None, :, None], dc_idx[None, :, :, None, :]].astype(jnp.float32)
    bias = jnp.where(col_valid[None, None, :, :, None, :], bias, NEG_INF)

    s_win = jnp.einsum('brjihd,brkjmhd->bhrjikm', qg, k_win,
                       preferred_element_type=jnp.float32) * scale + bias[None]
    s_meta = jnp.einsum('brjihd,bnhd->bhrjin', qg, km,
                        preferred_element_type=jnp.float32) * scale
    n_win = kh * KEY_COL_BLOCK
    s = jnp.concatenate([s_meta, s_win.reshape(B, H, rows, nqb, COL_BLOCK, n_win)], axis=-1)
    p = jax.nn.softmax(s, axis=-1).astype(v.dtype)
    p_meta = p[..., :N_META]
    p_win = p[..., N_META:].reshape(B, H, rows, nqb, COL_BLOCK, kh, KEY_COL_BLOCK)
    o_real = (jnp.einsum('bhrjikm,brkjmhd->brjihd', p_win, v_win)
              + jnp.einsum('bhrjin,bnhd->brjihd', p_meta, vm)).reshape(B, S, H, dh)

    s_mm = jnp.einsum('bnhd,bmhd->bhnm', qm, km, preferred_element_type=jnp.float32) * scale
    p_mm = jax.nn.softmax(s_mm, axis=-1).astype(v.dtype)
    o_meta = jnp.einsum('bhnm,bmhd->bnhd', p_mm, vm)
    return jnp.concatenate([o_meta, o_real], axis=1).reshape(B, L, H * dh)


def setup_inputs(seed: int = 0) -> dict:
    key = jax.random.key(seed)
    ks = jax.random.split(key, 16)
    f32 = jnp.float32
    n = lambda k, shape, s: jax.random.normal(k, shape, f32) * s
    return {
        "x": n(ks[0], (BATCH, SEQ, D_MODEL), 1.0),
        "meta_tokens": n(ks[1], (N_META, D_MODEL), 1.0),
        "norm1_g": 1.0 + n(ks[2], (DEPTH, D_MODEL), 0.01),
        "w_in": n(ks[3], (DEPTH, D_MODEL, PROJ_DIM), D_MODEL ** -0.5),
        "conv_w": n(ks[4], (DEPTH, CONV_K, CONV_DIM), CONV_K ** -0.5),
        "conv_b": n(ks[5], (DEPTH, CONV_DIM), 0.01),
        "conv_norm_g": 1.0 + n(ks[6], (DEPTH, CONV_DIM), 0.01),
        "attn_rpb": n(ks[7], (DEPTH, ATTN_HEADS, 2 * WIN_ROWS - 1, 2 * WIN_COLS - 1), 0.02),
        "attn_norm_g": 1.0 + n(ks[8], (DEPTH, ATTN_DIM), 0.01),
        "w_out": n(ks[9], (DEPTH, D_MODEL, D_MODEL), D_MODEL ** -0.5),
        "norm2_g": 1.0 + n(ks[10], (DEPTH, D_MODEL), 0.01),
        "w_up": n(ks[11], (DEPTH, D_MODEL, D_FF), D_MODEL ** -0.5),
        "w_down": n(ks[12], (DEPTH, D_FF, D_MODEL), D_FF ** -0.5),
        "final_norm_g": 1.0 + n(ks[13], (D_MODEL,), 0.01),
    }


def reference(x, meta_tokens, norm1_g, w_in, conv_w, conv_b, conv_norm_g, attn_rpb,
              attn_norm_g, w_out, norm2_g, w_up, w_down, final_norm_g):
    B = x.shape[0]
    meta = jnp.broadcast_to(meta_tokens[None].astype(x.dtype), (B, N_META, D_MODEL))
    h_res = jnp.concatenate([meta, x], axis=1)
    L = h_res.shape[1]
    splits = [CONV_DIM, 2 * CONV_DIM, 3 * CONV_DIM,
              3 * CONV_DIM + ATTN_DIM, 3 * CONV_DIM + 2 * ATTN_DIM]
    for l in range(DEPTH):
        h = rms_norm(h_res, norm1_g[l])
        proj = h @ w_in[l].astype(h.dtype)
        b_gate, c_gate, u, q, k, v = jnp.split(proj, splits, axis=-1)
        y_conv = short_conv_mixer(b_gate, c_gate, u, conv_w[l], conv_b[l])
        y_attn = neighbourhood_attention(
            q.reshape(B, L, ATTN_HEADS, HEAD_DIM), k.reshape(B, L, ATTN_HEADS, HEAD_DIM),
            v.reshape(B, L, ATTN_HEADS, HEAD_DIM), attn_rpb[l])
        mix = jnp.concatenate([rms_norm(y_conv, conv_norm_g[l]),
                               rms_norm(y_attn, attn_norm_g[l])], axis=-1)
        h_res = h_res + mix @ w_out[l].astype(mix.dtype)
        h = rms_norm(h_res, norm2_g[l])
        h_res = h_res + jnp.square(jax.nn.relu(h @ w_up[l].astype(h.dtype))) @ w_down[l].astype(h.dtype)
    out = rms_norm(h_res, final_norm_g)
    return out[:, N_META:]
```

```python
import functools

import numpy as np
import jax
import jax.numpy as jnp
from jax import lax
from jax.experimental import pallas as pl
from jax.experimental.pallas import tpu as pltpu

D_MODEL = 2048
N_META = 16
CONV_DIM = D_MODEL // 2
ATTN_HEADS = 16
HEAD_DIM = 64
ATTN_DIM = ATTN_HEADS * HEAD_DIM
PROJ_DIM = 3 * CONV_DIM + 3 * ATTN_DIM
D_FF = 4 * D_MODEL
GRID_W = 64
WIN_ROWS = 8
WIN_COLS = 16
RMS_EPS = 1e-6
NEG_INF = -1e30

PANEL = 1024
P_B, P_C, P_U, P_Q, P_K, P_V = range(6)

V7X_VMEM_LIMIT_BYTES = 56 * 1024 * 1024
BF16_SUBLANES = 16
LANES = 128

HEADS_PER_GROUP = 4
GROUP_W = HEADS_PER_GROUP * HEAD_DIM
N_GROUPS = ATTN_HEADS // HEADS_PER_GROUP
ROWS_PER_STEP = 8
WIN_KEYS = WIN_ROWS * GRID_W
META_PAD = LANES


def _rms(xf, g):
    return xf * lax.rsqrt(jnp.mean(xf * xf, axis=-1, keepdims=True) + RMS_EPS) * g


def _in_proj_kernel(x_ref, g_ref, w_ref, o_ref, hn_ref):
    @pl.when(pl.program_id(1) == 0)
    def _():
        hn_ref[...] = _rms(x_ref[...], g_ref[...]).astype(jnp.bfloat16)

    o_ref[...] = jnp.dot(hn_ref[...], w_ref[...],
                         preferred_element_type=jnp.float32).astype(o_ref.dtype)


def _in_proj(x2d, g, w_bf16, tm, tn):
    rows = x2d.shape[0]
    return pl.pallas_call(
        _in_proj_kernel,
        grid=(rows // tm, PROJ_DIM // tn),
        in_specs=[
            pl.BlockSpec((tm, D_MODEL), lambda i, j: (i, 0)),
            pl.BlockSpec((1, D_MODEL), lambda i, j: (0, 0)),
            pl.BlockSpec((D_MODEL, tn), lambda i, j: (0, j)),
        ],
        out_specs=pl.BlockSpec((tm, tn), lambda i, j: (i, j)),
        out_shape=jax.ShapeDtypeStruct((rows, PROJ_DIM), jnp.bfloat16),
        scratch_shapes=[pltpu.VMEM((tm, D_MODEL), jnp.bfloat16)],
        compiler_params=pltpu.CompilerParams(
            dimension_semantics=("arbitrary", "arbitrary"),
            vmem_limit_bytes=V7X_VMEM_LIMIT_BYTES),
        name="in_proj",
    )(x2d, g, w_bf16)


def _attn_kernel(q_ref, k_ref, v_ref, km_ref, vm_ref, bias_ref, g_ref, o_ref,
                 kmp_ref, vmp_ref):
    grp = pl.program_id(1)

    kmp_ref[...] = jnp.zeros_like(kmp_ref)
    vmp_ref[...] = jnp.zeros_like(vmp_ref)
    kmp_ref[0:N_META, :] = km_ref[...]
    vmp_ref[0:N_META, :] = vm_ref[...]

    row_head = lax.broadcasted_iota(jnp.int32, (GROUP_W, GROUP_W), 0) // HEAD_DIM
    col_head = lax.broadcasted_iota(jnp.int32, (GROUP_W, GROUP_W), 1) // HEAD_DIM
    diag = row_head == col_head
    out_head = lax.broadcasted_iota(jnp.int32, (GRID_W, GROUP_W), 1) // HEAD_DIM
    meta_bias = jnp.where(
        lax.broadcasted_iota(jnp.int32, (1, META_PAD), 1) < N_META, 0.0, NEG_INF)
    scale = HEAD_DIM ** -0.5

    def row_body(rl, carry):
        r = grp * ROWS_PER_STEP + rl
        rs = jnp.clip(r - WIN_ROWS // 2, 0, GRID_W - WIN_ROWS)
        case = r - rs
        q0 = pl.multiple_of(rl * GRID_W, GRID_W)
        k0 = pl.multiple_of(rs * GRID_W, GRID_W)
        outs = []
        for hg in range(N_GROUPS):
            lanes = slice(hg * GROUP_W, (hg + 1) * GROUP_W)
            qg = q_ref[pl.ds(q0, GRID_W), lanes] * scale
            qbd = jnp.where(diag, jnp.concatenate([qg] * HEADS_PER_GROUP, axis=0),
                            jnp.zeros((), jnp.bfloat16))
            kw = k_ref[pl.ds(k0, WIN_KEYS), lanes]
            vw = v_ref[pl.ds(k0, WIN_KEYS), lanes]
            s_w = lax.dot_general(qbd, kw, (((1,), (1,)), ((), ())),
                                  preferred_element_type=jnp.float32)
            s_w = s_w + bias_ref[case, hg * GROUP_W:(hg + 1) * GROUP_W, :]
            s_m = lax.dot_general(qbd, kmp_ref[:, lanes], (((1,), (1,)), ((), ())),
                                  preferred_element_type=jnp.float32) + meta_bias
            m = jnp.maximum(jnp.max(s_w, axis=-1, keepdims=True),
                            jnp.max(s_m, axis=-1, keepdims=True))
            p_w = jnp.exp(s_w - m)
            p_m = jnp.exp(s_m - m)
            denom = (jnp.sum(p_w, axis=-1, keepdims=True)
                     + jnp.sum(p_m, axis=-1, keepdims=True))
            o_all = (jnp.dot(p_w.astype(jnp.bfloat16), vw,
                             preferred_element_type=jnp.float32)
                     + jnp.dot(p_m.astype(jnp.bfloat16), vmp_ref[:, lanes],
                               preferred_element_type=jnp.float32))
            o_all = o_all / denom
            o_g = jnp.zeros((GRID_W, GROUP_W), jnp.float32)
            for h in range(HEADS_PER_GROUP):
                o_g = o_g + jnp.where(out_head == h,
                                      o_all[h * GRID_W:(h + 1) * GRID_W, :], 0.0)
            outs.append(o_g)
        ssq = sum(jnp.sum(o * o, axis=-1, keepdims=True) for o in outs)
        inv = lax.rsqrt(ssq / ATTN_DIM + RMS_EPS)
        for hg in range(N_GROUPS):
            lanes = slice(hg * GROUP_W, (hg + 1) * GROUP_W)
            o_ref[pl.ds(q0, GRID_W), lanes] = (
                outs[hg] * inv * g_ref[:, lanes]).astype(o_ref.dtype)
        return carry

    lax.fori_loop(0, ROWS_PER_STEP, row_body, 0)


def _attention(proj, proj_meta, bias, g, batch, seq):
    rows = seq // GRID_W
    steps = rows // ROWS_PER_STEP
    tq = ROWS_PER_STEP * GRID_W
    single = pl.Buffered(1)
    return pl.pallas_call(
        _attn_kernel,
        grid=(batch, steps),
        in_specs=[
            pl.BlockSpec((tq, PANEL), lambda b, s: (b * steps + s, P_Q)),
            pl.BlockSpec((seq, PANEL), lambda b, s: (b, P_K), pipeline_mode=single),
            pl.BlockSpec((seq, PANEL), lambda b, s: (b, P_V), pipeline_mode=single),
            pl.BlockSpec((N_META, PANEL), lambda b, s: (0, P_K)),
            pl.BlockSpec((N_META, PANEL), lambda b, s: (0, P_V)),
            pl.BlockSpec(bias.shape, lambda b, s: (0, 0, 0), pipeline_mode=single),
            pl.BlockSpec((1, ATTN_DIM), lambda b, s: (0, 0)),
        ],
        out_specs=pl.BlockSpec((tq, ATTN_DIM), lambda b, s: (b * steps + s, 0)),
        out_shape=jax.ShapeDtypeStruct((batch * seq, ATTN_DIM), jnp.bfloat16),
        scratch_shapes=[pltpu.VMEM((META_PAD, PANEL), jnp.bfloat16),
                        pltpu.VMEM((META_PAD, PANEL), jnp.bfloat16)],
        compiler_params=pltpu.CompilerParams(
            dimension_semantics=("arbitrary", "arbitrary"),
            vmem_limit_bytes=V7X_VMEM_LIMIT_BYTES),
        name="attention",
    )(proj, proj, proj, proj_meta, proj_meta, bias, g)


def _bias_table(rpb):
    case = np.arange(WIN_ROWS)[:, None, None, None]
    qc = np.arange(GRID_W)[None, :, None, None]
    kr = np.arange(WIN_ROWS)[None, None, :, None]
    kc = np.arange(GRID_W)[None, None, None, :]
    dr = np.broadcast_to(kr - case + (WIN_ROWS - 1), (WIN_ROWS, GRID_W, WIN_ROWS, GRID_W))
    dc = np.broadcast_to(np.clip(kc - qc + (WIN_COLS - 1), 0, 2 * WIN_COLS - 2), dr.shape)
    c0 = np.clip(qc - WIN_COLS // 2, 0, GRID_W - WIN_COLS)
    valid = np.broadcast_to((kc >= c0) & (kc < c0 + WIN_COLS), dr.shape)
    tbl = rpb[:, dr, dc].astype(jnp.float32)
    tbl = jnp.where(valid[None], tbl, NEG_INF)
    tbl = jnp.transpose(tbl, (1, 0, 2, 3, 4))
    return tbl.reshape(WIN_ROWS, ATTN_HEADS * GRID_W, WIN_KEYS)


def _mix_kernel(b_ref, c_ref, u_ref, cp_ref, up_ref, cn_ref, un_ref, cm_ref, um_ref,
                ya_ref, x_ref, cw_ref, cb_ref, cg_ref, wo_ref, o_ref, *, blocks_per_seq):
    i = pl.program_id(0)
    tm = b_ref.shape[0]
    f32 = jnp.float32
    v = c_ref[...].astype(f32) * u_ref[...].astype(f32)
    last = BF16_SUBLANES - 1
    v_before = cp_ref[last:, :].astype(f32) * up_ref[last:, :].astype(f32)
    v_meta = cm_ref[N_META - 1:, :].astype(f32) * um_ref[N_META - 1:, :].astype(f32)
    v_after = cn_ref[0:1, :].astype(f32) * un_ref[0:1, :].astype(f32)
    seq_first = (i % blocks_per_seq) == 0
    seq_last = (i % blocks_per_seq) == blocks_per_seq - 1
    v_before = jnp.where(seq_first, v_meta, v_before)
    v_after = jnp.where(seq_last, 0.0, v_after)
    t = lax.broadcasted_iota(jnp.int32, (tm, 1), 0)
    v_prev = jnp.where(t == 0, v_before, pltpu.roll(v, 1, axis=0))
    v_next = jnp.where(t == tm - 1, v_after, pltpu.roll(v, tm - 1, axis=0))
    conv = (v_prev * cw_ref[0:1, :] + v * cw_ref[1:2, :] + v_next * cw_ref[2:3, :]
            + cb_ref[...])
    y = b_ref[...].astype(f32) * conv
    yc = _rms(y, cg_ref[...]).astype(jnp.bfloat16)
    acc = jnp.dot(yc, wo_ref[0:CONV_DIM, :], preferred_element_type=f32)
    acc = acc + jnp.dot(ya_ref[...], wo_ref[CONV_DIM:, :], preferred_element_type=f32)
    o_ref[...] = x_ref[...] + acc


def _mix(proj, proj_meta, y_attn, x2d, conv_w, conv_b, conv_g, w_out_bf16, seq, tm):
    rows = x2d.shape[0]
    nblk = rows // tm
    hb = tm // BF16_SUBLANES
    n_hb = rows // BF16_SUBLANES
    panel = lambda p: pl.BlockSpec((tm, PANEL), lambda i, p=p: (i, p))
    prev = lambda p: pl.BlockSpec((BF16_SUBLANES, PANEL),
                                  lambda i, p=p: (jnp.maximum(i * hb - 1, 0), p))
    nxt = lambda p: pl.BlockSpec((BF16_SUBLANES, PANEL),
                                 lambda i, p=p: (jnp.minimum((i + 1) * hb, n_hb - 1), p))
    meta = lambda p: pl.BlockSpec((N_META, PANEL), lambda i, p=p: (0, p))
    vec = lambda n: pl.BlockSpec((n, CONV_DIM), lambda i: (0, 0))
    return pl.pallas_call(
        functools.partial(_mix_kernel, blocks_per_seq=seq // tm),
        grid=(nblk,),
        in_specs=[
            panel(P_B), panel(P_C), panel(P_U),
            prev(P_C), prev(P_U), nxt(P_C), nxt(P_U), meta(P_C), meta(P_U),
            pl.BlockSpec((tm, ATTN_DIM), lambda i: (i, 0)),
            pl.BlockSpec((tm, D_MODEL), lambda i: (i, 0)),
            vec(3), vec(1), vec(1),
            pl.BlockSpec((D_MODEL, D_MODEL), lambda i: (0, 0),
                         pipeline_mode=pl.Buffered(1)),
        ],
        out_specs=pl.BlockSpec((tm, D_MODEL), lambda i: (i, 0)),
        out_shape=jax.ShapeDtypeStruct((rows, D_MODEL), jnp.float32),
        compiler_params=pltpu.CompilerParams(
            dimension_semantics=("arbitrary",),
            vmem_limit_bytes=V7X_VMEM_LIMIT_BYTES),
        name="mix",
    )(proj, proj, proj, proj, proj, proj, proj, proj_meta, proj_meta,
      y_attn, x2d, conv_w, conv_b, conv_g, w_out_bf16)


def _mlp_kernel(h_ref, g2_ref, wu_ref, wd_ref, gf_ref, o_ref, hn_ref):
    f = pl.program_id(1)

    @pl.when(f == 0)
    def _():
        h = h_ref[...]
        hn_ref[...] = _rms(h, g2_ref[...]).astype(jnp.bfloat16)
        o_ref[...] = h

    a = jnp.dot(hn_ref[...], wu_ref[...], preferred_element_type=jnp.float32)
    a = jnp.square(jnp.maximum(a, 0.0)).astype(jnp.bfloat16)
    o_ref[...] += jnp.dot(a, wd_ref[...], preferred_element_type=jnp.float32)

    @pl.when(f == pl.num_programs(1) - 1)
    def _():
        o_ref[...] = _rms(o_ref[...], gf_ref[...])


def _mlp(h1, g2, w_up_bf16, w_down_bf16, gf, tm, tf):
    rows = h1.shape[0]
    return pl.pallas_call(
        _mlp_kernel,
        grid=(rows // tm, D_FF // tf),
        in_specs=[
            pl.BlockSpec((tm, D_MODEL), lambda i, f: (i, 0)),
            pl.BlockSpec((1, D_MODEL), lambda i, f: (0, 0)),
            pl.BlockSpec((D_MODEL, tf), lambda i, f: (0, f)),
            pl.BlockSpec((tf, D_MODEL), lambda i, f: (f, 0)),
            pl.BlockSpec((1, D_MODEL), lambda i, f: (0, 0)),
        ],
        out_specs=pl.BlockSpec((tm, D_MODEL), lambda i, f: (i, 0)),
        out_shape=jax.ShapeDtypeStruct((rows, D_MODEL), jnp.float32),
        scratch_shapes=[pltpu.VMEM((tm, D_MODEL), jnp.bfloat16)],
        compiler_params=pltpu.CompilerParams(
            dimension_semantics=("arbitrary", "arbitrary"),
            vmem_limit_bytes=V7X_VMEM_LIMIT_BYTES),
        name="mlp",
    )(h1, g2, w_up_bf16, w_down_bf16, gf)


def kernel(x, meta_tokens, norm1_g, w_in, conv_w, conv_b, conv_norm_g, attn_rpb,
           attn_norm_g, w_out, norm2_g, w_up, w_down, final_norm_g):
    batch, seq, _ = x.shape
    bf16 = jnp.bfloat16
    x2d = x.reshape(batch * seq, D_MODEL)
    w_in_b = w_in[0].astype(bf16)
    proj = _in_proj(x2d, norm1_g, w_in_b, tm=1024, tn=1024)
    proj_meta = _in_proj(meta_tokens, norm1_g, w_in_b, tm=N_META, tn=1024)
    bias = _bias_table(attn_rpb[0])
    y_attn = _attention(proj, proj_meta, bias, attn_norm_g, batch, seq)
    h1 = _mix(proj, proj_meta, y_attn, x2d, conv_w[0], conv_b, conv_norm_g,
              w_out[0].astype(bf16), seq, tm=512)
    out = _mlp(h1, norm2_g, w_up[0].astype(bf16), w_down[0].astype(bf16),
               final_norm_g.reshape(1, D_MODEL), tm=1024, tf=512)
    return out.reshape(batch, seq, D_MODEL)
```

```python
import functools

import jax
import jax.numpy as jnp
from jax import lax
from jax.experimental import pallas as pl
from jax.experimental.pallas import tpu as pltpu

D_MODEL = 2048
N_META = 16
CONV_DIM = D_MODEL // 2
ATTN_HEADS = 16
HEAD_DIM = 64
ATTN_DIM = ATTN_HEADS * HEAD_DIM
PROJ_DIM = 3 * CONV_DIM + 3 * ATTN_DIM
D_FF = 4 * D_MODEL
GRID_W = 64
WIN_ROWS = 8
WIN_COLS = 16
RMS_EPS = 1e-6
NEG_INF = -1e30

PANEL = 1024
P_B, P_C, P_U, P_Q, P_K, P_V = range(6)

V7X_VMEM_LIMIT_BYTES = 56 * 1024 * 1024
BF16_SUBLANES = 16
LANES = 128

HEADS_PER_GROUP = 4
GROUP_W = HEADS_PER_GROUP * HEAD_DIM
N_GROUPS = ATTN_HEADS // HEADS_PER_GROUP
ROWS_PER_STEP = 8
WIN_KEYS = WIN_ROWS * GRID_W
META_PAD = LANES


def _rms(xf, g):
    return xf * lax.rsqrt(jnp.mean(xf * xf, axis=-1, keepdims=True) + RMS_EPS) * g


def _in_proj_kernel(x_ref, g_ref, w_ref, o_ref, hn_ref):
    @pl.when(pl.program_id(1) == 0)
    def _():
        hn_ref[...] = _rms(x_ref[...], g_ref[...]).astype(jnp.bfloat16)

    o_ref[...] = jnp.dot(hn_ref[...], w_ref[...],
                         preferred_element_type=jnp.float32).astype(o_ref.dtype)


def _in_proj(x2d, g, w_bf16, tm, tn):
    rows = x2d.shape[0]
    return pl.pallas_call(
        _in_proj_kernel,
        grid=(rows // tm, PROJ_DIM // tn),
        in_specs=[
            pl.BlockSpec((tm, D_MODEL), lambda i, j: (i, 0)),
            pl.BlockSpec((1, D_MODEL), lambda i, j: (0, 0)),
            pl.BlockSpec((D_MODEL, tn), lambda i, j: (0, j)),
        ],
        out_specs=pl.BlockSpec((tm, tn), lambda i, j: (i, j)),
        out_shape=jax.ShapeDtypeStruct((rows, PROJ_DIM), jnp.bfloat16),
        scratch_shapes=[pltpu.VMEM((tm, D_MODEL), jnp.bfloat16)],
        compiler_params=pltpu.CompilerParams(
            dimension_semantics=("arbitrary", "arbitrary"),
            vmem_limit_bytes=V7X_VMEM_LIMIT_BYTES),
        name="in_proj",
    )(x2d, g, w_bf16)


def _attn_kernel(q_ref, k_ref, v_ref, km_ref, vm_ref, bias_ref, g_ref, o_ref,
                 kmp_ref, vmp_ref):
    grp = pl.program_id(1)

    kmp_ref[...] = jnp.zeros_like(kmp_ref)
    vmp_ref[...] = jnp.zeros_like(vmp_ref)
    kmp_ref[0:N_META, :] = km_ref[...]
    vmp_ref[0:N_META, :] = vm_ref[...]

    row_head = lax.broadcasted_iota(jnp.int32, (GROUP_W, GROUP_W), 0) // HEAD_DIM
    col_head = lax.broadcasted_iota(jnp.int32, (GROUP_W, GROUP_W), 1) // HEAD_DIM
    diag = row_head == col_head
    out_head = lax.broadcasted_iota(jnp.int32, (GRID_W, GROUP_W), 1) // HEAD_DIM
    meta_bias = jnp.where(
        lax.broadcasted_iota(jnp.int32, (1, META_PAD), 1) < N_META, 0.0, NEG_INF)
    scale = HEAD_DIM ** -0.5

    def row_body(rl, carry):
        r = grp * ROWS_PER_STEP + rl
        rs = jnp.clip(r - WIN_ROWS // 2, 0, GRID_W - WIN_ROWS)
        case = r - rs
        q0 = pl.multiple_of(rl * GRID_W, GRID_W)
        k0 = pl.multiple_of(rs * GRID_W, GRID_W)
        outs = []
        for hg in range(N_GROUPS):
            lanes = slice(hg * GROUP_W, (hg + 1) * GROUP_W)
            qg = q_ref[pl.ds(q0, GRID_W), lanes] * scale
            qbd = jnp.where(diag, jnp.concatenate([qg] * HEADS_PER_GROUP, axis=0),
                            jnp.zeros((), jnp.bfloat16))
            kw = k_ref[pl.ds(k0, WIN_KEYS), lanes]
            vw = v_ref[pl.ds(k0, WIN_KEYS), lanes]
            s_w = lax.dot_general(qbd, kw, (((1,), (1,)), ((), ())),
                                  preferred_element_type=jnp.float32)
            s_w = s_w + bias_ref[case, hg * GROUP_W:(hg + 1) * GROUP_W, :]
            s_m = lax.dot_general(qbd, kmp_ref[:, lanes], (((1,), (1,)), ((), ())),
                                  preferred_element_type=jnp.float32) + meta_bias
            m = jnp.maximum(jnp.max(s_w, axis=-1, keepdims=True),
                            jnp.max(s_m, axis=-1, keepdims=True))
            p_w = jnp.exp(s_w - m)
            p_m = jnp.exp(s_m - m)
            denom = (jnp.sum(p_w, axis=-1, keepdims=True)
                     + jnp.sum(p_m, axis=-1, keepdims=True))
            o_all = (jnp.dot(p_w.astype(jnp.bfloat16), vw,
                             preferred_element_type=jnp.float32)
                     + jnp.dot(p_m.astype(jnp.bfloat16), vmp_ref[:, lanes],
                               preferred_element_type=jnp.float32))
            o_all = o_all / denom
            o_g = jnp.zeros((GRID_W, GROUP_W), jnp.float32)
            for h in range(HEADS_PER_GROUP):
                o_g = o_g + jnp.where(out_head == h,
                                      o_all[h * GRID_W:(h + 1) * GRID_W, :], 0.0)
            outs.append(o_g)
        ssq = sum(jnp.sum(o * o, axis=-1, keepdims=True) for o in outs)
        inv = lax.rsqrt(ssq / ATTN_DIM + RMS_EPS)
        for hg in range(N_GROUPS):
            lanes = slice(hg * GROUP_W, (hg + 1) * GROUP_W)
            o_ref[pl.ds(q0, GRID_W), lanes] = (
                outs[hg] * inv * g_ref[:, lanes]).astype(o_ref.dtype)
        return carry

    lax.fori_loop(0, ROWS_PER_STEP, row_body, 0)


def _attention(proj, proj_meta, bias, g, batch, seq):
    rows = seq // GRID_W
    steps = rows // ROWS_PER_STEP
    tq = ROWS_PER_STEP * GRID_W
    single = pl.Buffered(1)
    return pl.pallas_call(
        _attn_kernel,
        grid=(batch, steps),
        in_specs=[
            pl.BlockSpec((tq, PANEL), lambda b, s: (b * steps + s, P_Q)),
            pl.BlockSpec((seq, PANEL), lambda b, s: (b, P_K), pipeline_mode=single),
            pl.BlockSpec((seq, PANEL), lambda b, s: (b, P_V), pipeline_mode=single),
            pl.BlockSpec((N_META, PANEL), lambda b, s: (0, P_K)),
            pl.BlockSpec((N_META, PANEL), lambda b, s: (0, P_V)),
            pl.BlockSpec(bias.shape, lambda b, s: (0, 0, 0), pipeline_mode=single),
            pl.BlockSpec((1, ATTN_DIM), lambda b, s: (0, 0)),
        ],
        out_specs=pl.BlockSpec((tq, ATTN_DIM), lambda b, s: (b * steps + s, 0)),
        out_shape=jax.ShapeDtypeStruct((batch * seq, ATTN_DIM), jnp.bfloat16),
        scratch_shapes=[pltpu.VMEM((META_PAD, PANEL), jnp.bfloat16),
                        pltpu.VMEM((META_PAD, PANEL), jnp.bfloat16)],
        compiler_params=pltpu.CompilerParams(
            dimension_semantics=("arbitrary", "arbitrary"),
            vmem_limit_bytes=V7X_VMEM_LIMIT_BYTES),
        name="attention",
    )(proj, proj, proj, proj_meta, proj_meta, bias, g)


N_DR = 2 * WIN_ROWS - 1
N_DC = 2 * WIN_COLS - 1


def _bias_kernel(rpb_ref, o_ref):
    h = pl.program_id(0)
    shape = (GRID_W, LANES)
    qc = lax.broadcasted_iota(jnp.int32, shape, 0)
    lane = lax.broadcasted_iota(jnp.int32, shape, 1)
    kc = lane % GRID_W
    dc = kc - qc + (WIN_COLS - 1)
    c0 = jnp.clip(qc - WIN_COLS // 2, 0, GRID_W - WIN_COLS)
    valid = (kc >= c0) & (kc < c0 + WIN_COLS)
    tiles = []
    for dr in range(N_DR):
        base = (h * N_DR + dr) * N_DC
        t = jnp.zeros(shape, jnp.float32)
        for j in range(N_DC):
            t = jnp.where(dc == j, rpb_ref[base + j], t)
        tiles.append(jnp.where(valid, t, NEG_INF))
    low_half = lane < GRID_W
    for case in range(WIN_ROWS):
        for p in range(WIN_ROWS // 2):
            dr0 = 2 * p - case + (WIN_ROWS - 1)
            o_ref[case, :, p * LANES:(p + 1) * LANES] = jnp.where(
                low_half, tiles[dr0], tiles[dr0 + 1])


def _bias_table(rpb):
    return pl.pallas_call(
        _bias_kernel,
        grid=(ATTN_HEADS,),
        in_specs=[pl.BlockSpec(memory_space=pltpu.SMEM)],
        out_specs=pl.BlockSpec((WIN_ROWS, GRID_W, WIN_KEYS), lambda h: (0, h, 0)),
        out_shape=jax.ShapeDtypeStruct((WIN_ROWS, ATTN_HEADS * GRID_W, WIN_KEYS),
                                       jnp.float32),
        compiler_params=pltpu.CompilerParams(dimension_semantics=("arbitrary",)),
        name="bias_table",
    )(rpb.reshape(-1))


def _mix_kernel(b_ref, c_ref, u_ref, cp_ref, up_ref, cn_ref, un_ref, cm_ref, um_ref,
                ya_ref, x_ref, cw_ref, cb_ref, cg_ref, wo_ref, o_ref, *, blocks_per_seq):
    i = pl.program_id(0)
    tm = b_ref.shape[0]
    f32 = jnp.float32
    v = c_ref[...].astype(f32) * u_ref[...].astype(f32)
    last = BF16_SUBLANES - 1
    v_before = cp_ref[last:, :].astype(f32) * up_ref[last:, :].astype(f32)
    v_meta = cm_ref[N_META - 1:, :].astype(f32) * um_ref[N_META - 1:, :].astype(f32)
    v_after = cn_ref[0:1, :].astype(f32) * un_ref[0:1, :].astype(f32)
    seq_first = (i % blocks_per_seq) == 0
    seq_last = (i % blocks_per_seq) == blocks_per_seq - 1
    v_before = jnp.where(seq_first, v_meta, v_before)
    v_after = jnp.where(seq_last, 0.0, v_after)
    t = lax.broadcasted_iota(jnp.int32, (tm, 1), 0)
    v_prev = jnp.where(t == 0, v_before, pltpu.roll(v, 1, axis=0))
    v_next = jnp.where(t == tm - 1, v_after, pltpu.roll(v, tm - 1, axis=0))
    conv = (v_prev * cw_ref[0:1, :] + v * cw_ref[1:2, :] + v_next * cw_ref[2:3, :]
            + cb_ref[...])
    y = b_ref[...].astype(f32) * conv
    yc = _rms(y, cg_ref[...]).astype(jnp.bfloat16)
    acc = jnp.dot(yc, wo_ref[0:CONV_DIM, :], preferred_element_type=f32)
    acc = acc + jnp.dot(ya_ref[...], wo_ref[CONV_DIM:, :], preferred_element_type=f32)
    o_ref[...] = x_ref[...] + acc


def _mix(proj, proj_meta, y_attn, x2d, conv_w, conv_b, conv_g, w_out_bf16, seq, tm):
    rows = x2d.shape[0]
    nblk = rows // tm
    hb = tm // BF16_SUBLANES
    n_hb = rows // BF16_SUBLANES
    panel = lambda p: pl.BlockSpec((tm, PANEL), lambda i, p=p: (i, p))
    prev = lambda p: pl.BlockSpec((BF16_SUBLANES, PANEL),
                                  lambda i, p=p: (jnp.maximum(i * hb - 1, 0), p))
    nxt = lambda p: pl.BlockSpec((BF16_SUBLANES, PANEL),
                                 lambda i, p=p: (jnp.minimum((i + 1) * hb, n_hb - 1), p))
    meta = lambda p: pl.BlockSpec((N_META, PANEL), lambda i, p=p: (0, p))
    vec = lambda n: pl.BlockSpec((n, CONV_DIM), lambda i: (0, 0))
    return pl.pallas_call(
        functools.partial(_mix_kernel, blocks_per_seq=seq // tm),
        grid=(nblk,),
        in_specs=[
            panel(P_B), panel(P_C), panel(P_U),
            prev(P_C), prev(P_U), nxt(P_C), nxt(P_U), meta(P_C), meta(P_U),
            pl.BlockSpec((tm, ATTN_DIM), lambda i: (i, 0)),
            pl.BlockSpec((tm, D_MODEL), lambda i: (i, 0)),
            vec(3), vec(1), vec(1),
            pl.BlockSpec((D_MODEL, D_MODEL), lambda i: (0, 0),
                         pipeline_mode=pl.Buffered(1)),
        ],
        out_specs=pl.BlockSpec((tm, D_MODEL), lambda i: (i, 0)),
        out_shape=jax.ShapeDtypeStruct((rows, D_MODEL), jnp.float32),
        compiler_params=pltpu.CompilerParams(
            dimension_semantics=("arbitrary",),
            vmem_limit_bytes=V7X_VMEM_LIMIT_BYTES),
        name="mix",
    )(proj, proj, proj, proj, proj, proj, proj, proj_meta, proj_meta,
      y_attn, x2d, conv_w, conv_b, conv_g, w_out_bf16)


def _mlp_kernel(h_ref, g2_ref, wu_ref, wd_ref, gf_ref, o_ref, hn_ref):
    f = pl.program_id(1)

    @pl.when(f == 0)
    def _():
        h = h_ref[...]
        hn_ref[...] = _rms(h, g2_ref[...]).astype(jnp.bfloat16)
        o_ref[...] = h

    a = jnp.dot(hn_ref[...], wu_ref[...], preferred_element_type=jnp.float32)
    a = jnp.square(jnp.maximum(a, 0.0)).astype(jnp.bfloat16)
    o_ref[...] += jnp.dot(a, wd_ref[...], preferred_element_type=jnp.float32)

    @pl.when(f == pl.num_programs(1) - 1)
    def _():
        o_ref[...] = _rms(o_ref[...], gf_ref[...])


def _mlp(h1, g2, w_up_bf16, w_down_bf16, gf, tm, tf):
    rows = h1.shape[0]
    return pl.pallas_call(
        _mlp_kernel,
        grid=(rows // tm, D_FF // tf),
        in_specs=[
            pl.BlockSpec((tm, D_MODEL), lambda i, f: (i, 0)),
            pl.BlockSpec((1, D_MODEL), lambda i, f: (0, 0)),
            pl.BlockSpec((D_MODEL, tf), lambda i, f: (0, f)),
            pl.BlockSpec((tf, D_MODEL), lambda i, f: (f, 0)),
            pl.BlockSpec((1, D_MODEL), lambda i, f: (0, 0)),
        ],
        out_specs=pl.BlockSpec((tm, D_MODEL), lambda i, f: (i, 0)),
        out_shape=jax.ShapeDtypeStruct((rows, D_MODEL), jnp.float32),
        scratch_shapes=[pltpu.VMEM((tm, D_MODEL), jnp.bfloat16)],
        compiler_params=pltpu.CompilerParams(
            dimension_semantics=("arbitrary", "arbitrary"),
            vmem_limit_bytes=V7X_VMEM_LIMIT_BYTES),
        name="mlp",
    )(h1, g2, w_up_bf16, w_down_bf16, gf)


def kernel(x, meta_tokens, norm1_g, w_in, conv_w, conv_b, conv_norm_g, attn_rpb,
           attn_norm_g, w_out, norm2_g, w_up, w_down, final_norm_g):
    batch, seq, _ = x.shape
    bf16 = jnp.bfloat16
    x2d = x.reshape(batch * seq, D_MODEL)
    w_in_b = w_in[0].astype(bf16)
    proj = _in_proj(x2d, norm1_g, w_in_b, tm=1024, tn=1024)
    proj_meta = _in_proj(meta_tokens, norm1_g, w_in_b, tm=N_META, tn=1024)
    bias = _bias_table(attn_rpb[0])
    y_attn = _attention(proj, proj_meta, bias, attn_norm_g, batch, seq)
    h1 = _mix(proj, proj_meta, y_attn, x2d, conv_w[0], conv_b, conv_norm_g,
              w_out[0].astype(bf16), seq, tm=512)
    out = _mlp(h1, norm2_g, w_up[0].astype(bf16), w_down[0].astype(bf16),
               final_norm_g.reshape(1, D_MODEL), tm=1024, tf=512)
    return out.reshape(batch, seq, D_MODEL)
```

```python
import functools

import jax
import jax.numpy as jnp
from jax import lax
from jax.experimental import pallas as pl
from jax.experimental.pallas import tpu as pltpu

D_MODEL = 2048
N_META = 16
CONV_DIM = D_MODEL // 2
ATTN_HEADS = 16
HEAD_DIM = 64
ATTN_DIM = ATTN_HEADS * HEAD_DIM
PROJ_DIM = 3 * CONV_DIM + 3 * ATTN_DIM
D_FF = 4 * D_MODEL
GRID_W = 64
WIN_ROWS = 8
WIN_COLS = 16
RMS_EPS = 1e-6
NEG_INF = -1e30

PANEL = 1024
P_B, P_C, P_U, P_Q, P_K, P_V = range(6)

V7X_VMEM_LIMIT_BYTES = 56 * 1024 * 1024
BF16_SUBLANES = 16
LANES = 128

HEADS_PER_GROUP = 4
GROUP_W = HEADS_PER_GROUP * HEAD_DIM
N_GROUPS = ATTN_HEADS // HEADS_PER_GROUP
ROWS_PER_STEP = 8
WIN_KEYS = WIN_ROWS * GRID_W
META_PAD = LANES
ALL_KEYS = WIN_KEYS + META_PAD
ROW_UNROLL = 8


def _rms(xf, g):
    return xf * lax.rsqrt(jnp.mean(xf * xf, axis=-1, keepdims=True) + RMS_EPS) * g


def _in_proj_kernel(x_ref, g_ref, w_ref, o_ref, hn_ref):
    @pl.when(pl.program_id(1) == 0)
    def _():
        hn_ref[...] = _rms(x_ref[...], g_ref[...]).astype(jnp.bfloat16)

    o_ref[...] = jnp.dot(hn_ref[...], w_ref[...],
                         preferred_element_type=jnp.float32).astype(o_ref.dtype)


def _in_proj(x2d, g, w_bf16, tm, tn):
    rows = x2d.shape[0]
    return pl.pallas_call(
        _in_proj_kernel,
        grid=(rows // tm, PROJ_DIM // tn),
        in_specs=[
            pl.BlockSpec((tm, D_MODEL), lambda i, j: (i, 0)),
            pl.BlockSpec((1, D_MODEL), lambda i, j: (0, 0)),
            pl.BlockSpec((D_MODEL, tn), lambda i, j: (0, j)),
        ],
        out_specs=pl.BlockSpec((tm, tn), lambda i, j: (i, j)),
        out_shape=jax.ShapeDtypeStruct((rows, PROJ_DIM), jnp.bfloat16),
        scratch_shapes=[pltpu.VMEM((tm, D_MODEL), jnp.bfloat16)],
        compiler_params=pltpu.CompilerParams(
            dimension_semantics=("arbitrary", "arbitrary"),
            vmem_limit_bytes=V7X_VMEM_LIMIT_BYTES),
        name="in_proj",
    )(x2d, g, w_bf16)


def _attn_kernel(q_ref, k_ref, v_ref, km_ref, vm_ref, bias_ref, g_ref, o_ref,
                 kmp_ref, vmp_ref):
    grp = pl.program_id(1)

    kmp_ref[...] = jnp.zeros_like(kmp_ref)
    vmp_ref[...] = jnp.zeros_like(vmp_ref)
    kmp_ref[0:N_META, :] = km_ref[...]
    vmp_ref[0:N_META, :] = vm_ref[...]

    row_head = lax.broadcasted_iota(jnp.int32, (GROUP_W, GROUP_W), 0) // HEAD_DIM
    col_head = lax.broadcasted_iota(jnp.int32, (GROUP_W, GROUP_W), 1) // HEAD_DIM
    diag = row_head == col_head
    lane = lax.broadcasted_iota(jnp.int32, (GRID_W, LANES), 1)
    low_half = lane < HEAD_DIM
    meta_mask = jnp.where(lane < N_META, 0.0, NEG_INF)
    scale = HEAD_DIM ** -0.5
    nt = (((1,), (1,)), ((), ()))

    def row_body(rl, carry):
        r = grp * ROWS_PER_STEP + rl
        rs = jnp.clip(r - WIN_ROWS // 2, 0, GRID_W - WIN_ROWS)
        case = r - rs
        q0 = pl.multiple_of(rl * GRID_W, GRID_W)
        k0 = pl.multiple_of(rs * GRID_W, GRID_W)
        outs = []
        for hg in range(N_GROUPS):
            lanes = slice(hg * GROUP_W, (hg + 1) * GROUP_W)
            qg = q_ref[pl.ds(q0, GRID_W), lanes] * scale
            qbd = jnp.where(diag, jnp.concatenate([qg] * HEADS_PER_GROUP, axis=0),
                            jnp.zeros((), jnp.bfloat16))
            kx = jnp.concatenate(
                [k_ref[pl.ds(k0, WIN_KEYS), lanes], kmp_ref[:, lanes]], axis=0)
            vx = jnp.concatenate(
                [v_ref[pl.ds(k0, WIN_KEYS), lanes], vmp_ref[:, lanes]], axis=0)
            s = lax.dot_general(qbd, kx, nt, preferred_element_type=jnp.float32)
            bias = jnp.concatenate([
                jnp.concatenate(
                    [bias_ref[hg * HEADS_PER_GROUP + h, 2 * p + (WIN_ROWS - 1) - case]
                     for p in range(WIN_ROWS // 2)] + [meta_mask], axis=1)
                for h in range(HEADS_PER_GROUP)], axis=0)
            s = s + bias
            p = jnp.exp(s - jnp.max(s, axis=-1, keepdims=True))
            inv_l = 1.0 / jnp.sum(p, axis=-1, keepdims=True)
            o_all = jnp.dot(p.astype(jnp.bfloat16), vx,
                            preferred_element_type=jnp.float32)
            tiles = []
            for t in range(GROUP_W // LANES):
                ra = slice(2 * t * GRID_W, (2 * t + 1) * GRID_W)
                rb = slice((2 * t + 1) * GRID_W, (2 * t + 2) * GRID_W)
                lt = slice(t * LANES, (t + 1) * LANES)
                tiles.append(jnp.where(low_half, o_all[ra, lt] * inv_l[ra],
                                       o_all[rb, lt] * inv_l[rb]))
            outs.append(jnp.concatenate(tiles, axis=1))
        ssq = sum(jnp.sum(o * o, axis=-1, keepdims=True) for o in outs)
        inv = lax.rsqrt(ssq / ATTN_DIM + RMS_EPS)
        for hg in range(N_GROUPS):
            lanes = slice(hg * GROUP_W, (hg + 1) * GROUP_W)
            o_ref[pl.ds(q0, GRID_W), lanes] = (
                outs[hg] * inv * g_ref[:, lanes]).astype(o_ref.dtype)
        return carry

    lax.fori_loop(0, ROWS_PER_STEP, row_body, 0, unroll=ROW_UNROLL)


def _attention(proj, proj_meta, bias, g, batch, seq):
    rows = seq // GRID_W
    steps = rows // ROWS_PER_STEP
    tq = ROWS_PER_STEP * GRID_W
    single = pl.Buffered(1)
    return pl.pallas_call(
        _attn_kernel,
        grid=(batch, steps),
        in_specs=[
            pl.BlockSpec((tq, PANEL), lambda b, s: (b * steps + s, P_Q)),
            pl.BlockSpec((seq, PANEL), lambda b, s: (b, P_K), pipeline_mode=single),
            pl.BlockSpec((seq, PANEL), lambda b, s: (b, P_V), pipeline_mode=single),
            pl.BlockSpec((N_META, PANEL), lambda b, s: (0, P_K)),
            pl.BlockSpec((N_META, PANEL), lambda b, s: (0, P_V)),
            pl.BlockSpec(bias.shape, lambda b, s: (0, 0, 0, 0), pipeline_mode=single),
            pl.BlockSpec((1, ATTN_DIM), lambda b, s: (0, 0)),
        ],
        out_specs=pl.BlockSpec((tq, ATTN_DIM), lambda b, s: (b * steps + s, 0)),
        out_shape=jax.ShapeDtypeStruct((batch * seq, ATTN_DIM), jnp.bfloat16),
        scratch_shapes=[pltpu.VMEM((META_PAD, PANEL), jnp.bfloat16),
                        pltpu.VMEM((META_PAD, PANEL), jnp.bfloat16)],
        compiler_params=pltpu.CompilerParams(
            dimension_semantics=("arbitrary", "arbitrary"),
            vmem_limit_bytes=V7X_VMEM_LIMIT_BYTES),
        name="attention",
    )(proj, proj, proj, proj_meta, proj_meta, bias, g)


N_DR = 2 * WIN_ROWS - 1
N_DC = 2 * WIN_COLS - 1


N_PAIR = N_DR - 1


def _bias_kernel(rpb_ref, o_ref):
    h = pl.program_id(0)
    shape = (GRID_W, LANES)
    qc = lax.broadcasted_iota(jnp.int32, shape, 0)
    lane = lax.broadcasted_iota(jnp.int32, shape, 1)
    kc = lane % GRID_W
    dc = kc - qc + (WIN_COLS - 1)
    c0 = jnp.clip(qc - WIN_COLS // 2, 0, GRID_W - WIN_COLS)
    valid = (kc >= c0) & (kc < c0 + WIN_COLS)
    tiles = []
    for dr in range(N_DR):
        base = (h * N_DR + dr) * N_DC
        t = jnp.zeros(shape, jnp.float32)
        for j in range(N_DC):
            t = jnp.where(dc == j, rpb_ref[base + j], t)
        tiles.append(jnp.where(valid, t, NEG_INF))
    low_half = lane < GRID_W
    for dr0 in range(N_PAIR):
        o_ref[0, dr0] = jnp.where(low_half, tiles[dr0], tiles[dr0 + 1])


def _bias_table(rpb):
    return pl.pallas_call(
        _bias_kernel,
        grid=(ATTN_HEADS,),
        in_specs=[pl.BlockSpec(memory_space=pltpu.SMEM)],
        out_specs=pl.BlockSpec((1, N_PAIR, GRID_W, LANES), lambda h: (h, 0, 0, 0)),
        out_shape=jax.ShapeDtypeStruct((ATTN_HEADS, N_PAIR, GRID_W, LANES), jnp.float32),
        compiler_params=pltpu.CompilerParams(dimension_semantics=("arbitrary",)),
        name="bias_table",
    )(rpb.reshape(-1))


def _mix_kernel(b_ref, c_ref, u_ref, cp_ref, up_ref, cn_ref, un_ref, cm_ref, um_ref,
                ya_ref, x_ref, cw_ref, cb_ref, cg_ref, wo_ref, o_ref, *, blocks_per_seq):
    i = pl.program_id(0)
    tm = b_ref.shape[0]
    f32 = jnp.float32
    v = c_ref[...].astype(f32) * u_ref[...].astype(f32)
    last = BF16_SUBLANES - 1
    v_before = cp_ref[last:, :].astype(f32) * up_ref[last:, :].astype(f32)
    v_meta = cm_ref[N_META - 1:, :].astype(f32) * um_ref[N_META - 1:, :].astype(f32)
    v_after = cn_ref[0:1, :].astype(f32) * un_ref[0:1, :].astype(f32)
    seq_first = (i % blocks_per_seq) == 0
    seq_last = (i % blocks_per_seq) == blocks_per_seq - 1
    v_before = jnp.where(seq_first, v_meta, v_before)
    v_after = jnp.where(seq_last, 0.0, v_after)
    t = lax.broadcasted_iota(jnp.int32, (tm, 1), 0)
    v_prev = jnp.where(t == 0, v_before, pltpu.roll(v, 1, axis=0))
    v_next = jnp.where(t == tm - 1, v_after, pltpu.roll(v, tm - 1, axis=0))
    conv = (v_prev * cw_ref[0:1, :] + v * cw_ref[1:2, :] + v_next * cw_ref[2:3, :]
            + cb_ref[...])
    y = b_ref[...].astype(f32) * conv
    yc = _rms(y, cg_ref[...]).astype(jnp.bfloat16)
    acc = jnp.dot(yc, wo_ref[0:CONV_DIM, :], preferred_element_type=f32)
    acc = acc + jnp.dot(ya_ref[...], wo_ref[CONV_DIM:, :], preferred_element_type=f32)
    o_ref[...] = x_ref[...] + acc


def _mix(proj, proj_meta, y_attn, x2d, conv_w, conv_b, conv_g, w_out_bf16, seq, tm):
    rows = x2d.shape[0]
    nblk = rows // tm
    hb = tm // BF16_SUBLANES
    n_hb = rows // BF16_SUBLANES
    panel = lambda p: pl.BlockSpec((tm, PANEL), lambda i, p=p: (i, p))
    prev = lambda p: pl.BlockSpec((BF16_SUBLANES, PANEL),
                                  lambda i, p=p: (jnp.maximum(i * hb - 1, 0), p))
    nxt = lambda p: pl.BlockSpec((BF16_SUBLANES, PANEL),
                                 lambda i, p=p: (jnp.minimum((i + 1) * hb, n_hb - 1), p))
    meta = lambda p: pl.BlockSpec((N_META, PANEL), lambda i, p=p: (0, p))
    vec = lambda n: pl.BlockSpec((n, CONV_DIM), lambda i: (0, 0))
    return pl.pallas_call(
        functools.partial(_mix_kernel, blocks_per_seq=seq // tm),
        grid=(nblk,),
        in_specs=[
            panel(P_B), panel(P_C), panel(P_U),
            prev(P_C), prev(P_U), nxt(P_C), nxt(P_U), meta(P_C), meta(P_U),
            pl.BlockSpec((tm, ATTN_DIM), lambda i: (i, 0)),
            pl.BlockSpec((tm, D_MODEL), lambda i: (i, 0)),
            vec(3), vec(1), vec(1),
            pl.BlockSpec((D_MODEL, D_MODEL), lambda i: (0, 0),
                         pipeline_mode=pl.Buffered(1)),
        ],
        out_specs=pl.BlockSpec((tm, D_MODEL), lambda i: (i, 0)),
        out_shape=jax.ShapeDtypeStruct((rows, D_MODEL), jnp.float32),
        compiler_params=pltpu.CompilerParams(
            dimension_semantics=("arbitrary",),
            vmem_limit_bytes=V7X_VMEM_LIMIT_BYTES),
        name="mix",
    )(proj, proj, proj, proj, proj, proj, proj, proj_meta, proj_meta,
      y_attn, x2d, conv_w, conv_b, conv_g, w_out_bf16)


def _mlp_kernel(h_ref, g2_ref, wu_ref, wd_ref, gf_ref, o_ref, hn_ref):
    f = pl.program_id(1)

    @pl.when(f == 0)
    def _():
        h = h_ref[...]
        hn_ref[...] = _rms(h, g2_ref[...]).astype(jnp.bfloat16)
        o_ref[...] = h

    a = jnp.dot(hn_ref[...], wu_ref[...], preferred_element_type=jnp.float32)
    a = jnp.square(jnp.maximum(a, 0.0)).astype(jnp.bfloat16)
    o_ref[...] += jnp.dot(a, wd_ref[...], preferred_element_type=jnp.float32)

    @pl.when(f == pl.num_programs(1) - 1)
    def _():
        o_ref[...] = _rms(o_ref[...], gf_ref[...])


def _mlp(h1, g2, w_up_bf16, w_down_bf16, gf, tm, tf):
    rows = h1.shape[0]
    return pl.pallas_call(
        _mlp_kernel,
        grid=(rows // tm, D_FF // tf),
        in_specs=[
            pl.BlockSpec((tm, D_MODEL), lambda i, f: (i, 0)),
            pl.BlockSpec((1, D_MODEL), lambda i, f: (0, 0)),
            pl.BlockSpec((D_MODEL, tf), lambda i, f: (0, f)),
            pl.BlockSpec((tf, D_MODEL), lambda i, f: (f, 0)),
            pl.BlockSpec((1, D_MODEL), lambda i, f: (0, 0)),
        ],
        out_specs=pl.BlockSpec((tm, D_MODEL), lambda i, f: (i, 0)),
        out_shape=jax.ShapeDtypeStruct((rows, D_MODEL), jnp.float32),
        scratch_shapes=[pltpu.VMEM((tm, D_MODEL), jnp.bfloat16)],
        compiler_params=pltpu.CompilerParams(
            dimension_semantics=("arbitrary", "arbitrary"),
            vmem_limit_bytes=V7X_VMEM_LIMIT_BYTES),
        name="mlp",
    )(h1, g2, w_up_bf16, w_down_bf16, gf)


def kernel(x, meta_tokens, norm1_g, w_in, conv_w, conv_b, conv_norm_g, attn_rpb,
           attn_norm_g, w_out, norm2_g, w_up, w_down, final_norm_g):
    batch, seq, _ = x.shape
    bf16 = jnp.bfloat16
    x2d = x.reshape(batch * seq, D_MODEL)
    w_in_b = w_in[0].astype(bf16)
    proj = _in_proj(x2d, norm1_g, w_in_b, tm=1024, tn=1024)
    proj_meta = _in_proj(meta_tokens, norm1_g, w_in_b, tm=N_META, tn=1024)
    bias = _bias_table(attn_rpb[0])
    y_attn = _attention(proj, proj_meta, bias, attn_norm_g, batch, seq)
    h1 = _mix(proj, proj_meta, y_attn, x2d, conv_w[0], conv_b, conv_norm_g,
              w_out[0].astype(bf16), seq, tm=512)
    out = _mlp(h1, norm2_g, w_up[0].astype(bf16), w_down[0].astype(bf16),
               final_norm_g.reshape(1, D_MODEL), tm=1024, tf=512)
    return out.reshape(batch, seq, D_MODEL)
```

```python
import functools

import jax
import jax.numpy as jnp
from jax import lax
from jax.experimental import pallas as pl
from jax.experimental.pallas import tpu as pltpu

D_MODEL = 2048
N_META = 16
CONV_DIM = D_MODEL // 2
ATTN_HEADS = 16
HEAD_DIM = 64
ATTN_DIM = ATTN_HEADS * HEAD_DIM
PROJ_DIM = 3 * CONV_DIM + 3 * ATTN_DIM
D_FF = 4 * D_MODEL
GRID_W = 64
WIN_ROWS = 8
WIN_COLS = 16
RMS_EPS = 1e-6
NEG_INF = -1e30

PANEL = 1024
P_B, P_C, P_U, P_Q, P_K, P_V = range(6)

V7X_VMEM_LIMIT_BYTES = 56 * 1024 * 1024
BF16_SUBLANES = 16
LANES = 128

HEADS_PER_GROUP = 4
GROUP_W = HEADS_PER_GROUP * HEAD_DIM
N_GROUPS = ATTN_HEADS // HEADS_PER_GROUP
ROWS_PER_STEP = 8
WIN_KEYS = WIN_ROWS * GRID_W
META_PAD = LANES
ALL_KEYS = WIN_KEYS + META_PAD
ROW_UNROLL = 8


def _rms(xf, g):
    return xf * lax.rsqrt(jnp.mean(xf * xf, axis=-1, keepdims=True) + RMS_EPS) * g


def _in_proj_first_kernel(x_ref, meta_ref, g_ref, w_ref, o_ref, om_ref, wb_ref, hn_ref):
    tm = x_ref.shape[0]

    @pl.when(pl.program_id(0) == 0)
    def _():
        hn_ref[0:tm, :] = _rms(x_ref[...], g_ref[...]).astype(jnp.bfloat16)
        hn_ref[tm:, :] = _rms(meta_ref[...], g_ref[...]).astype(jnp.bfloat16)

    w = w_ref[...].astype(jnp.bfloat16)
    wb_ref[...] = w
    r = jnp.dot(hn_ref[...], w, preferred_element_type=jnp.float32)
    o_ref[...] = r[0:tm].astype(o_ref.dtype)
    om_ref[...] = r[tm:].astype(om_ref.dtype)


def _in_proj_rest_kernel(x_ref, g_ref, w_ref, proj_hbm_ref, o_ref, hn_ref):
    del proj_hbm_ref
    @pl.when(pl.program_id(1) == 0)
    def _():
        hn_ref[...] = _rms(x_ref[...], g_ref[...]).astype(jnp.bfloat16)

    o_ref[...] = jnp.dot(hn_ref[...], w_ref[...],
                         preferred_element_type=jnp.float32).astype(o_ref.dtype)


def _in_proj(x2d, meta, g, w_f32, tm, tn):
    rows = x2d.shape[0]
    nj = PROJ_DIM // tn
    params = lambda sem: pltpu.CompilerParams(
        dimension_semantics=sem, vmem_limit_bytes=V7X_VMEM_LIMIT_BYTES)
    proj, proj_meta, w_bf16 = pl.pallas_call(
        _in_proj_first_kernel,
        grid=(nj,),
        in_specs=[
            pl.BlockSpec((tm, D_MODEL), lambda j: (0, 0)),
            pl.BlockSpec((N_META, D_MODEL), lambda j: (0, 0)),
            pl.BlockSpec((1, D_MODEL), lambda j: (0, 0)),
            pl.BlockSpec((D_MODEL, tn), lambda j: (0, j)),
        ],
        out_specs=[
            pl.BlockSpec((tm, tn), lambda j: (0, j)),
            pl.BlockSpec((N_META, tn), lambda j: (0, j)),
            pl.BlockSpec((D_MODEL, tn), lambda j: (0, j)),
        ],
        out_shape=[
            jax.ShapeDtypeStruct((rows, PROJ_DIM), jnp.bfloat16),
            jax.ShapeDtypeStruct((N_META, PROJ_DIM), jnp.bfloat16),
            jax.ShapeDtypeStruct((D_MODEL, PROJ_DIM), jnp.bfloat16),
        ],
        scratch_shapes=[pltpu.VMEM((tm + N_META, D_MODEL), jnp.bfloat16)],
        compiler_params=params(("arbitrary",)),
        name="in_proj_first",
    )(x2d, meta, g, w_f32)
    proj = pl.pallas_call(
        _in_proj_rest_kernel,
        grid=(rows // tm - 1, nj),
        in_specs=[
            pl.BlockSpec((tm, D_MODEL), lambda i, j: (i + 1, 0)),
            pl.BlockSpec((1, D_MODEL), lambda i, j: (0, 0)),
            pl.BlockSpec((D_MODEL, tn), lambda i, j: (0, j)),
            pl.BlockSpec(memory_space=pl.ANY),
        ],
        out_specs=pl.BlockSpec((tm, tn), lambda i, j: (i + 1, j)),
        out_shape=jax.ShapeDtypeStruct((rows, PROJ_DIM), jnp.bfloat16),
        scratch_shapes=[pltpu.VMEM((tm, D_MODEL), jnp.bfloat16)],
        input_output_aliases={3: 0},
        compiler_params=params(("arbitrary", "arbitrary")),
        name="in_proj_rest",
    )(x2d, g, w_bf16, proj)
    return proj, proj_meta


def _attn_kernel(q_ref, k_ref, v_ref, km_ref, vm_ref, bias_ref, g_ref, o_ref,
                 kmp_ref, vmp_ref):
    grp = pl.program_id(1)

    kmp_ref[...] = jnp.zeros_like(kmp_ref)
    vmp_ref[...] = jnp.zeros_like(vmp_ref)
    kmp_ref[0:N_META, :] = km_ref[...]
    vmp_ref[0:N_META, :] = vm_ref[...]

    row_head = lax.broadcasted_iota(jnp.int32, (GROUP_W, GROUP_W), 0) // HEAD_DIM
    col_head = lax.broadcasted_iota(jnp.int32, (GROUP_W, GROUP_W), 1) // HEAD_DIM
    diag = row_head == col_head
    lane = lax.broadcasted_iota(jnp.int32, (GRID_W, LANES), 1)
    low_half = lane < HEAD_DIM
    meta_mask = jnp.where(lane < N_META, 0.0, NEG_INF)
    scale = HEAD_DIM ** -0.5
    nt = (((1,), (1,)), ((), ()))

    def row_body(rl, carry):
        r = grp * ROWS_PER_STEP + rl
        rs = jnp.clip(r - WIN_ROWS // 2, 0, GRID_W - WIN_ROWS)
        case = r - rs
        q0 = pl.multiple_of(rl * GRID_W, GRID_W)
        k0 = pl.multiple_of(rs * GRID_W, GRID_W)
        outs = []
        for hg in range(N_GROUPS):
            lanes = slice(hg * GROUP_W, (hg + 1) * GROUP_W)
            qg = q_ref[pl.ds(q0, GRID_W), lanes] * scale
            qbd = jnp.where(diag, jnp.concatenate([qg] * HEADS_PER_GROUP, axis=0),
                            jnp.zeros((), jnp.bfloat16))
            kx = jnp.concatenate(
                [k_ref[pl.ds(k0, WIN_KEYS), lanes], kmp_ref[:, lanes]], axis=0)
            vx = jnp.concatenate(
                [v_ref[pl.ds(k0, WIN_KEYS), lanes], vmp_ref[:, lanes]], axis=0)
            s = lax.dot_general(qbd, kx, nt, preferred_element_type=jnp.float32)
            bias = jnp.concatenate([
                jnp.concatenate(
                    [bias_ref[hg * HEADS_PER_GROUP + h, 2 * p + (WIN_ROWS - 1) - case]
                     for p in range(WIN_ROWS // 2)] + [meta_mask], axis=1)
                for h in range(HEADS_PER_GROUP)], axis=0)
            s = s + bias
            p = jnp.exp(s - jnp.max(s, axis=-1, keepdims=True))
            inv_l = 1.0 / jnp.sum(p, axis=-1, keepdims=True)
            o_all = jnp.dot(p.astype(jnp.bfloat16), vx,
                            preferred_element_type=jnp.float32)
            tiles = []
            for t in range(GROUP_W // LANES):
                ra = slice(2 * t * GRID_W, (2 * t + 1) * GRID_W)
                rb = slice((2 * t + 1) * GRID_W, (2 * t + 2) * GRID_W)
                lt = slice(t * LANES, (t + 1) * LANES)
                tiles.append(jnp.where(low_half, o_all[ra, lt] * inv_l[ra],
                                       o_all[rb, lt] * inv_l[rb]))
            outs.append(jnp.concatenate(tiles, axis=1))
        ssq = sum(jnp.sum(o * o, axis=-1, keepdims=True) for o in outs)
        inv = lax.rsqrt(ssq / ATTN_DIM + RMS_EPS)
        for hg in range(N_GROUPS):
            lanes = slice(hg * GROUP_W, (hg + 1) * GROUP_W)
            o_ref[pl.ds(q0, GRID_W), lanes] = (
                outs[hg] * inv * g_ref[:, lanes]).astype(o_ref.dtype)
        return carry

    lax.fori_loop(0, ROWS_PER_STEP, row_body, 0, unroll=ROW_UNROLL)


def _attention(proj, proj_meta, bias, g, batch, seq):
    rows = seq // GRID_W
    steps = rows // ROWS_PER_STEP
    tq = ROWS_PER_STEP * GRID_W
    single = pl.Buffered(1)
    return pl.pallas_call(
        _attn_kernel,
        grid=(batch, steps),
        in_specs=[
            pl.BlockSpec((tq, PANEL), lambda b, s: (b * steps + s, P_Q)),
            pl.BlockSpec((seq, PANEL), lambda b, s: (b, P_K), pipeline_mode=single),
            pl.BlockSpec((seq, PANEL), lambda b, s: (b, P_V), pipeline_mode=single),
            pl.BlockSpec((N_META, PANEL), lambda b, s: (0, P_K)),
            pl.BlockSpec((N_META, PANEL), lambda b, s: (0, P_V)),
            pl.BlockSpec(bias.shape, lambda b, s: (0, 0, 0, 0), pipeline_mode=single),
            pl.BlockSpec((1, ATTN_DIM), lambda b, s: (0, 0)),
        ],
        out_specs=pl.BlockSpec((tq, ATTN_DIM), lambda b, s: (b * steps + s, 0)),
        out_shape=jax.ShapeDtypeStruct((batch * seq, ATTN_DIM), jnp.bfloat16),
        scratch_shapes=[pltpu.VMEM((META_PAD, PANEL), jnp.bfloat16),
                        pltpu.VMEM((META_PAD, PANEL), jnp.bfloat16)],
        compiler_params=pltpu.CompilerParams(
            dimension_semantics=("arbitrary", "arbitrary"),
            vmem_limit_bytes=V7X_VMEM_LIMIT_BYTES),
        name="attention",
    )(proj, proj, proj, proj_meta, proj_meta, bias, g)


N_DR = 2 * WIN_ROWS - 1
N_DC = 2 * WIN_COLS - 1


N_PAIR = N_DR - 1


def _bias_kernel(rpb_ref, o_ref):
    h = pl.program_id(0)
    shape = (GRID_W, LANES)
    qc = lax.broadcasted_iota(jnp.int32, shape, 0)
    lane = lax.broadcasted_iota(jnp.int32, shape, 1)
    kc = lane % GRID_W
    dc = kc - qc + (WIN_COLS - 1)
    c0 = jnp.clip(qc - WIN_COLS // 2, 0, GRID_W - WIN_COLS)
    valid = (kc >= c0) & (kc < c0 + WIN_COLS)
    tiles = []
    for dr in range(N_DR):
        base = (h * N_DR + dr) * N_DC
        t = jnp.zeros(shape, jnp.float32)
        for j in range(N_DC):
            t = jnp.where(dc == j, rpb_ref[base + j], t)
        tiles.append(jnp.where(valid, t, NEG_INF))
    low_half = lane < GRID_W
    for dr0 in range(N_PAIR):
        o_ref[0, dr0] = jnp.where(low_half, tiles[dr0], tiles[dr0 + 1])


def _bias_table(rpb):
    return pl.pallas_call(
        _bias_kernel,
        grid=(ATTN_HEADS,),
        in_specs=[pl.BlockSpec(memory_space=pltpu.SMEM)],
        out_specs=pl.BlockSpec((1, N_PAIR, GRID_W, LANES), lambda h: (h, 0, 0, 0)),
        out_shape=jax.ShapeDtypeStruct((ATTN_HEADS, N_PAIR, GRID_W, LANES), jnp.float32),
        compiler_params=pltpu.CompilerParams(dimension_semantics=("arbitrary",)),
        name="bias_table",
    )(rpb.reshape(-1))


def _mix_kernel(b_ref, c_ref, u_ref, cp_ref, up_ref, cn_ref, un_ref, cm_ref, um_ref,
                ya_ref, x_ref, cw_ref, cb_ref, cg_ref, wo_ref, o_ref, *, blocks_per_seq):
    i = pl.program_id(0)
    tm = b_ref.shape[0]
    f32 = jnp.float32
    v = c_ref[...].astype(f32) * u_ref[...].astype(f32)
    last = BF16_SUBLANES - 1
    v_before = cp_ref[last:, :].astype(f32) * up_ref[last:, :].astype(f32)
    v_meta = cm_ref[N_META - 1:, :].astype(f32) * um_ref[N_META - 1:, :].astype(f32)
    v_after = cn_ref[0:1, :].astype(f32) * un_ref[0:1, :].astype(f32)
    seq_first = (i % blocks_per_seq) == 0
    seq_last = (i % blocks_per_seq) == blocks_per_seq - 1
    v_before = jnp.where(seq_first, v_meta, v_before)
    v_after = jnp.where(seq_last, 0.0, v_after)
    t = lax.broadcasted_iota(jnp.int32, (tm, 1), 0)
    v_prev = jnp.where(t == 0, v_before, pltpu.roll(v, 1, axis=0))
    v_next = jnp.where(t == tm - 1, v_after, pltpu.roll(v, tm - 1, axis=0))
    conv = (v_prev * cw_ref[0:1, :] + v * cw_ref[1:2, :] + v_next * cw_ref[2:3, :]
            + cb_ref[...])
    y = b_ref[...].astype(f32) * conv
    yc = _rms(y, cg_ref[...]).astype(jnp.bfloat16)
    acc = jnp.dot(yc, wo_ref[0:CONV_DIM, :], preferred_element_type=f32)
    acc = acc + jnp.dot(ya_ref[...], wo_ref[CONV_DIM:, :], preferred_element_type=f32)
    o_ref[...] = x_ref[...] + acc


def _mix(proj, proj_meta, y_attn, x2d, conv_w, conv_b, conv_g, w_out_bf16, seq, tm):
    rows = x2d.shape[0]
    nblk = rows // tm
    hb = tm // BF16_SUBLANES
    n_hb = rows // BF16_SUBLANES
    panel = lambda p: pl.BlockSpec((tm, PANEL), lambda i, p=p: (i, p))
    prev = lambda p: pl.BlockSpec((BF16_SUBLANES, PANEL),
                                  lambda i, p=p: (jnp.maximum(i * hb - 1, 0), p))
    nxt = lambda p: pl.BlockSpec((BF16_SUBLANES, PANEL),
                                 lambda i, p=p: (jnp.minimum((i + 1) * hb, n_hb - 1), p))
    meta = lambda p: pl.BlockSpec((N_META, PANEL), lambda i, p=p: (0, p))
    vec = lambda n: pl.BlockSpec((n, CONV_DIM), lambda i: (0, 0))
    return pl.pallas_call(
        functools.partial(_mix_kernel, blocks_per_seq=seq // tm),
        grid=(nblk,),
        in_specs=[
            panel(P_B), panel(P_C), panel(P_U),
            prev(P_C), prev(P_U), nxt(P_C), nxt(P_U), meta(P_C), meta(P_U),
            pl.BlockSpec((tm, ATTN_DIM), lambda i: (i, 0)),
            pl.BlockSpec((tm, D_MODEL), lambda i: (i, 0)),
            vec(3), vec(1), vec(1),
            pl.BlockSpec((D_MODEL, D_MODEL), lambda i: (0, 0),
                         pipeline_mode=pl.Buffered(1)),
        ],
        out_specs=pl.BlockSpec((tm, D_MODEL), lambda i: (i, 0)),
        out_shape=jax.ShapeDtypeStruct((rows, D_MODEL), jnp.float32),
        compiler_params=pltpu.CompilerParams(
            dimension_semantics=("arbitrary",),
            vmem_limit_bytes=V7X_VMEM_LIMIT_BYTES),
        name="mix",
    )(proj, proj, proj, proj, proj, proj, proj, proj_meta, proj_meta,
      y_attn, x2d, conv_w, conv_b, conv_g, w_out_bf16)


def _mlp_step(f, nf, h_ref, g2_ref, wu, wd, gf_ref, o_ref, hn_ref):
    @pl.when(f == 0)
    def _():
        h = h_ref[...]
        hn_ref[...] = _rms(h, g2_ref[...]).astype(jnp.bfloat16)
        o_ref[...] = h

    a = jnp.dot(hn_ref[...], wu, preferred_element_type=jnp.float32)
    a = jnp.square(jnp.maximum(a, 0.0)).astype(jnp.bfloat16)
    o_ref[...] += jnp.dot(a, wd, preferred_element_type=jnp.float32)

    @pl.when(f == nf - 1)
    def _():
        o_ref[...] = _rms(o_ref[...], gf_ref[...])


def _mlp_first_kernel(h_ref, g2_ref, wu_ref, wd_ref, gf_ref, o_ref, wub_ref, wdb_ref, hn_ref):
    wu = wu_ref[...].astype(jnp.bfloat16)
    wd = wd_ref[...].astype(jnp.bfloat16)
    wub_ref[...] = wu
    wdb_ref[...] = wd
    _mlp_step(pl.program_id(0), pl.num_programs(0),
              h_ref, g2_ref, wu, wd, gf_ref, o_ref, hn_ref)


def _mlp_rest_kernel(h_ref, g2_ref, wu_ref, wd_ref, gf_ref, out_hbm_ref, o_ref, hn_ref):
    del out_hbm_ref
    _mlp_step(pl.program_id(1), pl.num_programs(1),
              h_ref, g2_ref, wu_ref[...], wd_ref[...], gf_ref, o_ref, hn_ref)


def _mlp(h1, g2, w_up_f32, w_down_f32, gf, tm, tf_first, tf):
    rows = h1.shape[0]
    vec = lambda nd: pl.BlockSpec((1, D_MODEL), (lambda f: (0, 0)) if nd == 1
                                  else (lambda i, f: (0, 0)))
    params = lambda sem: pltpu.CompilerParams(
        dimension_semantics=sem, vmem_limit_bytes=V7X_VMEM_LIMIT_BYTES)
    out, w_up_b, w_down_b = pl.pallas_call(
        _mlp_first_kernel,
        grid=(D_FF // tf_first,),
        in_specs=[
            pl.BlockSpec((tm, D_MODEL), lambda f: (0, 0)),
            vec(1),
            pl.BlockSpec((D_MODEL, tf_first), lambda f: (0, f)),
            pl.BlockSpec((tf_first, D_MODEL), lambda f: (f, 0)),
            vec(1),
        ],
        out_specs=[
            pl.BlockSpec((tm, D_MODEL), lambda f: (0, 0)),
            pl.BlockSpec((D_MODEL, tf_first), lambda f: (0, f)),
            pl.BlockSpec((tf_first, D_MODEL), lambda f: (f, 0)),
        ],
        out_shape=[
            jax.ShapeDtypeStruct((rows, D_MODEL), jnp.float32),
            jax.ShapeDtypeStruct((D_MODEL, D_FF), jnp.bfloat16),
            jax.ShapeDtypeStruct((D_FF, D_MODEL), jnp.bfloat16),
        ],
        scratch_shapes=[pltpu.VMEM((tm, D_MODEL), jnp.bfloat16)],
        compiler_params=params(("arbitrary",)),
        name="mlp_first",
    )(h1, g2, w_up_f32, w_down_f32, gf)
    return pl.pallas_call(
        _mlp_rest_kernel,
        grid=(rows // tm - 1, D_FF // tf),
        in_specs=[
            pl.BlockSpec((tm, D_MODEL), lambda i, f: (i + 1, 0)),
            vec(2),
            pl.BlockSpec((D_MODEL, tf), lambda i, f: (0, f)),
            pl.BlockSpec((tf, D_MODEL), lambda i, f: (f, 0)),
            vec(2),
            pl.BlockSpec(memory_space=pl.ANY),
        ],
        out_specs=pl.BlockSpec((tm, D_MODEL), lambda i, f: (i + 1, 0)),
        out_shape=jax.ShapeDtypeStruct((rows, D_MODEL), jnp.float32),
        scratch_shapes=[pltpu.VMEM((tm, D_MODEL), jnp.bfloat16)],
        input_output_aliases={5: 0},
        compiler_params=params(("arbitrary", "arbitrary")),
        name="mlp_rest",
    )(h1, g2, w_up_b, w_down_b, gf, out)


def kernel(x, meta_tokens, norm1_g, w_in, conv_w, conv_b, conv_norm_g, attn_rpb,
           attn_norm_g, w_out, norm2_g, w_up, w_down, final_norm_g):
    batch, seq, _ = x.shape
    x2d = x.reshape(batch * seq, D_MODEL)
    proj, proj_meta = _in_proj(x2d, meta_tokens, norm1_g, w_in[0], tm=1024, tn=1024)
    bias = _bias_table(attn_rpb[0])
    y_attn = _attention(proj, proj_meta, bias, attn_norm_g, batch, seq)
    h1 = _mix(proj, proj_meta, y_attn, x2d, conv_w[0], conv_b, conv_norm_g,
              w_out[0].astype(jnp.bfloat16), seq, tm=512)
    out = _mlp(h1, norm2_g, w_up[0], w_down[0], final_norm_g.reshape(1, D_MODEL),
               tm=1024, tf_first=256, tf=512)
    return out.reshape(batch, seq, D_MODEL)
```

```python
import functools

import jax
import jax.numpy as jnp
from jax import lax
from jax.experimental import pallas as pl
from jax.experimental.pallas import tpu as pltpu

D_MODEL = 2048
N_META = 16
CONV_DIM = D_MODEL // 2
ATTN_HEADS = 16
HEAD_DIM = 64
ATTN_DIM = ATTN_HEADS * HEAD_DIM
PROJ_DIM = 3 * CONV_DIM + 3 * ATTN_DIM
D_FF = 4 * D_MODEL
GRID_W = 64
WIN_ROWS = 8
WIN_COLS = 16
RMS_EPS = 1e-6
NEG_INF = -1e30

PANEL = 1024
P_B, P_C, P_U, P_Q, P_K, P_V = range(6)

V7X_VMEM_LIMIT_BYTES = 56 * 1024 * 1024
BF16_SUBLANES = 16
LANES = 128

HEADS_PER_GROUP = 4
GROUP_W = HEADS_PER_GROUP * HEAD_DIM
N_GROUPS = ATTN_HEADS // HEADS_PER_GROUP
ROWS_PER_STEP = 8
WIN_KEYS = WIN_ROWS * GRID_W
META_PAD = LANES
ALL_KEYS = WIN_KEYS + META_PAD
ROW_UNROLL = 8


def _rms(xf, g):
    return xf * lax.rsqrt(jnp.mean(xf * xf, axis=-1, keepdims=True) + RMS_EPS) * g


def _in_proj_first_kernel(x_ref, meta_ref, g_ref, w_ref, o_ref, om_ref, wb_ref, hn_ref):
    tm = x_ref.shape[0]

    @pl.when(pl.program_id(0) == 0)
    def _():
        hn_ref[0:tm, :] = _rms(x_ref[...], g_ref[...]).astype(jnp.bfloat16)
        hn_ref[tm:, :] = _rms(meta_ref[...], g_ref[...]).astype(jnp.bfloat16)

    w = w_ref[...].astype(jnp.bfloat16)
    wb_ref[...] = w
    r = jnp.dot(hn_ref[...], w, preferred_element_type=jnp.float32)
    o_ref[...] = r[0:tm].astype(o_ref.dtype)
    om_ref[...] = r[tm:].astype(om_ref.dtype)


def _in_proj_rest_kernel(x_ref, g_ref, w_ref, proj_hbm_ref, wu_ref, wd_ref, wo_ref,
                         o_ref, wub_ref, wdb_ref, wob_ref, hn_ref):
    del proj_hbm_ref
    @pl.when(pl.program_id(1) == 0)
    def _():
        hn_ref[...] = _rms(x_ref[...], g_ref[...]).astype(jnp.bfloat16)

    o_ref[...] = jnp.dot(hn_ref[...], w_ref[...],
                         preferred_element_type=jnp.float32).astype(o_ref.dtype)
    wub_ref[...] = wu_ref[...].astype(jnp.bfloat16)
    wdb_ref[...] = wd_ref[...].astype(jnp.bfloat16)
    wob_ref[...] = wo_ref[...].astype(jnp.bfloat16)


CAST_STEPS = 32


def _in_proj(x2d, meta, g, w_f32, w_up, w_down, w_out, tm, tn):
    rows = x2d.shape[0]
    nj = PROJ_DIM // tn
    ni = rows // tm - 1
    assert ni * nj >= CAST_STEPS
    slab = lambda i, j: jnp.minimum(i * nj + j, CAST_STEPS - 1)
    up_spec = pl.BlockSpec((D_MODEL, D_FF // CAST_STEPS), lambda i, j: (0, slab(i, j)))
    down_spec = pl.BlockSpec((D_FF // CAST_STEPS, D_MODEL), lambda i, j: (slab(i, j), 0))
    out_spec = pl.BlockSpec((D_MODEL // CAST_STEPS, D_MODEL), lambda i, j: (slab(i, j), 0))
    params = lambda sem: pltpu.CompilerParams(
        dimension_semantics=sem, vmem_limit_bytes=V7X_VMEM_LIMIT_BYTES)
    proj, proj_meta, w_bf16 = pl.pallas_call(
        _in_proj_first_kernel,
        grid=(nj,),
        in_specs=[
            pl.BlockSpec((tm, D_MODEL), lambda j: (0, 0)),
            pl.BlockSpec((N_META, D_MODEL), lambda j: (0, 0)),
            pl.BlockSpec((1, D_MODEL), lambda j: (0, 0)),
            pl.BlockSpec((D_MODEL, tn), lambda j: (0, j)),
        ],
        out_specs=[
            pl.BlockSpec((tm, tn), lambda j: (0, j)),
            pl.BlockSpec((N_META, tn), lambda j: (0, j)),
            pl.BlockSpec((D_MODEL, tn), lambda j: (0, j)),
        ],
        out_shape=[
            jax.ShapeDtypeStruct((rows, PROJ_DIM), jnp.bfloat16),
            jax.ShapeDtypeStruct((N_META, PROJ_DIM), jnp.bfloat16),
            jax.ShapeDtypeStruct((D_MODEL, PROJ_DIM), jnp.bfloat16),
        ],
        scratch_shapes=[pltpu.VMEM((tm + N_META, D_MODEL), jnp.bfloat16)],
        compiler_params=params(("arbitrary",)),
        name="in_proj_first",
    )(x2d, meta, g, w_f32)
    proj, w_up_b, w_down_b, w_out_b = pl.pallas_call(
        _in_proj_rest_kernel,
        grid=(ni, nj),
        in_specs=[
            pl.BlockSpec((tm, D_MODEL), lambda i, j: (i + 1, 0)),
            pl.BlockSpec((1, D_MODEL), lambda i, j: (0, 0)),
            pl.BlockSpec((D_MODEL, tn), lambda i, j: (0, j)),
            pl.BlockSpec(memory_space=pl.ANY),
            up_spec, down_spec, out_spec,
        ],
        out_specs=[pl.BlockSpec((tm, tn), lambda i, j: (i + 1, j)),
                   up_spec, down_spec, out_spec],
        out_shape=[jax.ShapeDtypeStruct((rows, PROJ_DIM), jnp.bfloat16),
                   jax.ShapeDtypeStruct(w_up.shape, jnp.bfloat16),
                   jax.ShapeDtypeStruct(w_down.shape, jnp.bfloat16),
                   jax.ShapeDtypeStruct(w_out.shape, jnp.bfloat16)],
        scratch_shapes=[pltpu.VMEM((tm, D_MODEL), jnp.bfloat16)],
        input_output_aliases={3: 0},
        compiler_params=params(("arbitrary", "arbitrary")),
        name="in_proj_rest",
    )(x2d, g, w_bf16, proj, w_up, w_down, w_out)
    return proj, proj_meta, w_up_b, w_down_b, w_out_b


def _attn_kernel(q_ref, k_ref, v_ref, km_ref, vm_ref, bias_ref, g_ref, o_ref,
                 kmp_ref, vmp_ref):
    grp = pl.program_id(1)

    kmp_ref[...] = jnp.zeros_like(kmp_ref)
    vmp_ref[...] = jnp.zeros_like(vmp_ref)
    kmp_ref[0:N_META, :] = km_ref[...]
    vmp_ref[0:N_META, :] = vm_ref[...]

    row_head = lax.broadcasted_iota(jnp.int32, (GROUP_W, GROUP_W), 0) // HEAD_DIM
    col_head = lax.broadcasted_iota(jnp.int32, (GROUP_W, GROUP_W), 1) // HEAD_DIM
    diag = row_head == col_head
    lane = lax.broadcasted_iota(jnp.int32, (GRID_W, LANES), 1)
    low_half = lane < HEAD_DIM
    meta_mask = jnp.where(lane < N_META, 0.0, NEG_INF)
    scale = HEAD_DIM ** -0.5
    nt = (((1,), (1,)), ((), ()))

    def row_body(rl, carry):
        r = grp * ROWS_PER_STEP + rl
        rs = jnp.clip(r - WIN_ROWS // 2, 0, GRID_W - WIN_ROWS)
        case = r - rs
        q0 = pl.multiple_of(rl * GRID_W, GRID_W)
        k0 = pl.multiple_of(rs * GRID_W, GRID_W)
        outs = []
        for hg in range(N_GROUPS):
            lanes = slice(hg * GROUP_W, (hg + 1) * GROUP_W)
            qg = q_ref[pl.ds(q0, GRID_W), lanes] * scale
            qbd = jnp.where(diag, jnp.concatenate([qg] * HEADS_PER_GROUP, axis=0),
                            jnp.zeros((), jnp.bfloat16))
            kx = jnp.concatenate(
                [k_ref[pl.ds(k0, WIN_KEYS), lanes], kmp_ref[:, lanes]], axis=0)
            vx = jnp.concatenate(
                [v_ref[pl.ds(k0, WIN_KEYS), lanes], vmp_ref[:, lanes]], axis=0)
            s = lax.dot_general(qbd, kx, nt, preferred_element_type=jnp.float32)
            bias = jnp.concatenate([
                jnp.concatenate(
                    [bias_ref[hg * HEADS_PER_GROUP + h, 2 * p + (WIN_ROWS - 1) - case]
                     for p in range(WIN_ROWS // 2)] + [meta_mask], axis=1)
                for h in range(HEADS_PER_GROUP)], axis=0)
            s = s + bias
            p = jnp.exp(s - jnp.max(s, axis=-1, keepdims=True))
            inv_l = 1.0 / jnp.sum(p, axis=-1, keepdims=True)
            o_all = jnp.dot(p.astype(jnp.bfloat16), vx,
                            preferred_element_type=jnp.float32)
            tiles = []
            for t in range(GROUP_W // LANES):
                ra = slice(2 * t * GRID_W, (2 * t + 1) * GRID_W)
                rb = slice((2 * t + 1) * GRID_W, (2 * t + 2) * GRID_W)
                lt = slice(t * LANES, (t + 1) * LANES)
                tiles.append(jnp.where(low_half, o_all[ra, lt] * inv_l[ra],
                                       o_all[rb, lt] * inv_l[rb]))
            outs.append(jnp.concatenate(tiles, axis=1))
        ssq = sum(jnp.sum(o * o, axis=-1, keepdims=True) for o in outs)
        inv = lax.rsqrt(ssq / ATTN_DIM + RMS_EPS)
        for hg in range(N_GROUPS):
            lanes = slice(hg * GROUP_W, (hg + 1) * GROUP_W)
            o_ref[pl.ds(q0, GRID_W), lanes] = (
                outs[hg] * inv * g_ref[:, lanes]).astype(o_ref.dtype)
        return carry

    lax.fori_loop(0, ROWS_PER_STEP, row_body, 0, unroll=ROW_UNROLL)


def _attention(proj, proj_meta, bias, g, batch, seq):
    rows = seq // GRID_W
    steps = rows // ROWS_PER_STEP
    tq = ROWS_PER_STEP * GRID_W
    single = pl.Buffered(1)
    return pl.pallas_call(
        _attn_kernel,
        grid=(batch, steps),
        in_specs=[
            pl.BlockSpec((tq, PANEL), lambda b, s: (b * steps + s, P_Q)),
            pl.BlockSpec((seq, PANEL), lambda b, s: (b, P_K), pipeline_mode=single),
            pl.BlockSpec((seq, PANEL), lambda b, s: (b, P_V), pipeline_mode=single),
            pl.BlockSpec((N_META, PANEL), lambda b, s: (0, P_K)),
            pl.BlockSpec((N_META, PANEL), lambda b, s: (0, P_V)),
            pl.BlockSpec(bias.shape, lambda b, s: (0, 0, 0, 0), pipeline_mode=single),
            pl.BlockSpec((1, ATTN_DIM), lambda b, s: (0, 0)),
        ],
        out_specs=pl.BlockSpec((tq, ATTN_DIM), lambda b, s: (b * steps + s, 0)),
        out_shape=jax.ShapeDtypeStruct((batch * seq, ATTN_DIM), jnp.bfloat16),
        scratch_shapes=[pltpu.VMEM((META_PAD, PANEL), jnp.bfloat16),
                        pltpu.VMEM((META_PAD, PANEL), jnp.bfloat16)],
        compiler_params=pltpu.CompilerParams(
            dimension_semantics=("arbitrary", "arbitrary"),
            vmem_limit_bytes=V7X_VMEM_LIMIT_BYTES),
        name="attention",
    )(proj, proj, proj, proj_meta, proj_meta, bias, g)


N_DR = 2 * WIN_ROWS - 1
N_DC = 2 * WIN_COLS - 1


N_PAIR = N_DR - 1


def _bias_kernel(rpb_ref, o_ref):
    h = pl.program_id(0)
    shape = (GRID_W, LANES)
    qc = lax.broadcasted_iota(jnp.int32, shape, 0)
    lane = lax.broadcasted_iota(jnp.int32, shape, 1)
    kc = lane % GRID_W
    dc = kc - qc + (WIN_COLS - 1)
    c0 = jnp.clip(qc - WIN_COLS // 2, 0, GRID_W - WIN_COLS)
    valid = (kc >= c0) & (kc < c0 + WIN_COLS)
    tiles = []
    for dr in range(N_DR):
        base = (h * N_DR + dr) * N_DC
        t = jnp.zeros(shape, jnp.float32)
        for j in range(N_DC):
            t = jnp.where(dc == j, rpb_ref[base + j], t)
        tiles.append(jnp.where(valid, t, NEG_INF))
    low_half = lane < GRID_W
    for dr0 in range(N_PAIR):
        o_ref[0, dr0] = jnp.where(low_half, tiles[dr0], tiles[dr0 + 1])


def _bias_table(rpb):
    return pl.pallas_call(
        _bias_kernel,
        grid=(ATTN_HEADS,),
        in_specs=[pl.BlockSpec(memory_space=pltpu.SMEM)],
        out_specs=pl.BlockSpec((1, N_PAIR, GRID_W, LANES), lambda h: (h, 0, 0, 0)),
        out_shape=jax.ShapeDtypeStruct((ATTN_HEADS, N_PAIR, GRID_W, LANES), jnp.float32),
        compiler_params=pltpu.CompilerParams(dimension_semantics=("arbitrary",)),
        name="bias_table",
    )(rpb.reshape(-1))


def _mix_kernel(b_ref, c_ref, u_ref, cp_ref, up_ref, cn_ref, un_ref, cm_ref, um_ref,
                ya_ref, x_ref, cw_ref, cb_ref, cg_ref, wo_ref, o_ref, *, blocks_per_seq):
    i = pl.program_id(0)
    tm = b_ref.shape[0]
    f32 = jnp.float32
    v = c_ref[...].astype(f32) * u_ref[...].astype(f32)
    last = BF16_SUBLANES - 1
    v_before = cp_ref[last:, :].astype(f32) * up_ref[last:, :].astype(f32)
    v_meta = cm_ref[N_META - 1:, :].astype(f32) * um_ref[N_META - 1:, :].astype(f32)
    v_after = cn_ref[0:1, :].astype(f32) * un_ref[0:1, :].astype(f32)
    seq_first = (i % blocks_per_seq) == 0
    seq_last = (i % blocks_per_seq) == blocks_per_seq - 1
    v_before = jnp.where(seq_first, v_meta, v_before)
    v_after = jnp.where(seq_last, 0.0, v_after)
    t = lax.broadcasted_iota(jnp.int32, (tm, 1), 0)
    v_prev = jnp.where(t == 0, v_before, pltpu.roll(v, 1, axis=0))
    v_next = jnp.where(t == tm - 1, v_after, pltpu.roll(v, tm - 1, axis=0))
    conv = (v_prev * cw_ref[0:1, :] + v * cw_ref[1:2, :] + v_next * cw_ref[2:3, :]
            + cb_ref[...])
    y = b_ref[...].astype(f32) * conv
    yc = _rms(y, cg_ref[...]).astype(jnp.bfloat16)
    acc = jnp.dot(yc, wo_ref[0:CONV_DIM, :], preferred_element_type=f32)
    acc = acc + jnp.dot(ya_ref[...], wo_ref[CONV_DIM:, :], preferred_element_type=f32)
    o_ref[...] = x_ref[...] + acc


def _mix(proj, proj_meta, y_attn, x2d, conv_w, conv_b, conv_g, w_out_bf16, seq, tm):
    rows = x2d.shape[0]
    nblk = rows // tm
    hb = tm // BF16_SUBLANES
    n_hb = rows // BF16_SUBLANES
    panel = lambda p: pl.BlockSpec((tm, PANEL), lambda i, p=p: (i, p))
    prev = lambda p: pl.BlockSpec((BF16_SUBLANES, PANEL),
                                  lambda i, p=p: (jnp.maximum(i * hb - 1, 0), p))
    nxt = lambda p: pl.BlockSpec((BF16_SUBLANES, PANEL),
                                 lambda i, p=p: (jnp.minimum((i + 1) * hb, n_hb - 1), p))
    meta = lambda p: pl.BlockSpec((N_META, PANEL), lambda i, p=p: (0, p))
    vec = lambda n: pl.BlockSpec((n, CONV_DIM), lambda i: (0, 0))
    return pl.pallas_call(
        functools.partial(_mix_kernel, blocks_per_seq=seq // tm),
        grid=(nblk,),
        in_specs=[
            panel(P_B), panel(P_C), panel(P_U),
            prev(P_C), prev(P_U), nxt(P_C), nxt(P_U), meta(P_C), meta(P_U),
            pl.BlockSpec((tm, ATTN_DIM), lambda i: (i, 0)),
            pl.BlockSpec((tm, D_MODEL), lambda i: (i, 0)),
            vec(3), vec(1), vec(1),
            pl.BlockSpec((D_MODEL, D_MODEL), lambda i: (0, 0),
                         pipeline_mode=pl.Buffered(1)),
        ],
        out_specs=pl.BlockSpec((tm, D_MODEL), lambda i: (i, 0)),
        out_shape=jax.ShapeDtypeStruct((rows, D_MODEL), jnp.float32),
        compiler_params=pltpu.CompilerParams(
            dimension_semantics=("arbitrary",),
            vmem_limit_bytes=V7X_VMEM_LIMIT_BYTES),
        name="mix",
    )(proj, proj, proj, proj, proj, proj, proj, proj_meta, proj_meta,
      y_attn, x2d, conv_w, conv_b, conv_g, w_out_bf16)


def _mlp_kernel(h_ref, g2_ref, wu_ref, wd_ref, gf_ref, o_ref, hn_ref):
    f = pl.program_id(1)

    @pl.when(f == 0)
    def _():
        h = h_ref[...]
        hn_ref[...] = _rms(h, g2_ref[...]).astype(jnp.bfloat16)
        o_ref[...] = h

    a = jnp.dot(hn_ref[...], wu_ref[...], preferred_element_type=jnp.float32)
    a = jnp.square(jnp.maximum(a, 0.0)).astype(jnp.bfloat16)
    o_ref[...] += jnp.dot(a, wd_ref[...], preferred_element_type=jnp.float32)

    @pl.when(f == pl.num_programs(1) - 1)
    def _():
        o_ref[...] = _rms(o_ref[...], gf_ref[...])


def _mlp(h1, g2, w_up_bf16, w_down_bf16, gf, tm, tf):
    rows = h1.shape[0]
    return pl.pallas_call(
        _mlp_kernel,
        grid=(rows // tm, D_FF // tf),
        in_specs=[
            pl.BlockSpec((tm, D_MODEL), lambda i, f: (i, 0)),
            pl.BlockSpec((1, D_MODEL), lambda i, f: (0, 0)),
            pl.BlockSpec((D_MODEL, tf), lambda i, f: (0, f)),
            pl.BlockSpec((tf, D_MODEL), lambda i, f: (f, 0)),
            pl.BlockSpec((1, D_MODEL), lambda i, f: (0, 0)),
        ],
        out_specs=pl.BlockSpec((tm, D_MODEL), lambda i, f: (i, 0)),
        out_shape=jax.ShapeDtypeStruct((rows, D_MODEL), jnp.float32),
        scratch_shapes=[pltpu.VMEM((tm, D_MODEL), jnp.bfloat16)],
        compiler_params=pltpu.CompilerParams(
            dimension_semantics=("arbitrary", "arbitrary"),
            vmem_limit_bytes=V7X_VMEM_LIMIT_BYTES),
        name="mlp",
    )(h1, g2, w_up_bf16, w_down_bf16, gf)


def kernel(x, meta_tokens, norm1_g, w_in, conv_w, conv_b, conv_norm_g, attn_rpb,
           attn_norm_g, w_out, norm2_g, w_up, w_down, final_norm_g):
    batch, seq, _ = x.shape
    x2d = x.reshape(batch * seq, D_MODEL)
    proj, proj_meta, w_up_b, w_down_b, w_out_b = _in_proj(
        x2d, meta_tokens, norm1_g, w_in[0], w_up[0], w_down[0], w_out[0], tm=1024, tn=1024)
    bias = _bias_table(attn_rpb[0])
    y_attn = _attention(proj, proj_meta, bias, attn_norm_g, batch, seq)
    h1 = _mix(proj, proj_meta, y_attn, x2d, conv_w[0], conv_b, conv_norm_g, w_out_b,
              seq, tm=512)
    out = _mlp(h1, norm2_g, w_up_b, w_down_b, final_norm_g.reshape(1, D_MODEL),
               tm=1024, tf=512)
    return out.reshape(batch, seq, D_MODEL)
```

```python
import functools

import jax
import jax.numpy as jnp
from jax import lax
from jax.experimental import pallas as pl
from jax.experimental.pallas import tpu as pltpu

D_MODEL = 2048
N_META = 16
CONV_DIM = D_MODEL // 2
ATTN_HEADS = 16
HEAD_DIM = 64
ATTN_DIM = ATTN_HEADS * HEAD_DIM
PROJ_DIM = 3 * CONV_DIM + 3 * ATTN_DIM
D_FF = 4 * D_MODEL
GRID_W = 64
WIN_ROWS = 8
WIN_COLS = 16
RMS_EPS = 1e-6
NEG_INF = -1e30

PANEL = 1024
P_B, P_C, P_U, P_Q, P_K, P_V = range(6)

V7X_VMEM_LIMIT_BYTES = 56 * 1024 * 1024
BF16_SUBLANES = 16
LANES = 128

HEADS_PER_GROUP = 4
GROUP_W = HEADS_PER_GROUP * HEAD_DIM
N_GROUPS = ATTN_HEADS // HEADS_PER_GROUP
ROWS_PER_STEP = 8
WIN_KEYS = WIN_ROWS * GRID_W
META_PAD = LANES
ALL_KEYS = WIN_KEYS + META_PAD
ROW_UNROLL = 8


def _rms(xf, g):
    return xf * lax.rsqrt(jnp.mean(xf * xf, axis=-1, keepdims=True) + RMS_EPS) * g


def _in_proj_first_kernel(x_ref, meta_ref, g_ref, w_ref, o_ref, om_ref, wb_ref, hn_ref):
    tm = x_ref.shape[0]

    @pl.when(pl.program_id(0) == 0)
    def _():
        hn_ref[0:tm, :] = _rms(x_ref[...], g_ref[...]).astype(jnp.bfloat16)
        hn_ref[tm:, :] = _rms(meta_ref[...], g_ref[...]).astype(jnp.bfloat16)

    w = w_ref[...].astype(jnp.bfloat16)
    wb_ref[...] = w
    r = jnp.dot(hn_ref[...], w, preferred_element_type=jnp.float32)
    o_ref[...] = r[0:tm].astype(o_ref.dtype)
    om_ref[...] = r[tm:].astype(om_ref.dtype)


def _in_proj_rest_kernel(x_ref, g_ref, w_ref, proj_hbm_ref, wo_ref, o_ref, wob_ref, hn_ref):
    del proj_hbm_ref
    @pl.when(pl.program_id(1) == 0)
    def _():
        hn_ref[...] = _rms(x_ref[...], g_ref[...]).astype(jnp.bfloat16)

    o_ref[...] = jnp.dot(hn_ref[...], w_ref[...],
                         preferred_element_type=jnp.float32).astype(o_ref.dtype)
    wob_ref[...] = wo_ref[...].astype(jnp.bfloat16)


CAST_STEPS = 32


def _in_proj(x2d, meta, g, w_f32, w_out, tm):
    rows = x2d.shape[0]
    nj = PROJ_DIM // PANEL
    ni = rows // tm - 1
    assert ni * nj >= CAST_STEPS
    slab = lambda i, j: jnp.minimum(i * nj + j, CAST_STEPS - 1)
    out_spec = pl.BlockSpec((D_MODEL // CAST_STEPS, D_MODEL), lambda i, j: (slab(i, j), 0))
    params = lambda sem: pltpu.CompilerParams(
        dimension_semantics=sem, vmem_limit_bytes=V7X_VMEM_LIMIT_BYTES)
    proj, proj_meta, w_bf16 = pl.pallas_call(
        _in_proj_first_kernel,
        grid=(nj,),
        in_specs=[
            pl.BlockSpec((tm, D_MODEL), lambda j: (0, 0)),
            pl.BlockSpec((N_META, D_MODEL), lambda j: (0, 0)),
            pl.BlockSpec((1, D_MODEL), lambda j: (0, 0)),
            pl.BlockSpec((D_MODEL, PANEL), lambda j: (0, j)),
        ],
        out_specs=[
            pl.BlockSpec((None, tm, PANEL), lambda j: (j, 0, 0)),
            pl.BlockSpec((None, N_META, PANEL), lambda j: (j, 0, 0)),
            pl.BlockSpec((None, D_MODEL, PANEL), lambda j: (j, 0, 0)),
        ],
        out_shape=[
            jax.ShapeDtypeStruct((nj, rows, PANEL), jnp.bfloat16),
            jax.ShapeDtypeStruct((nj, N_META, PANEL), jnp.bfloat16),
            jax.ShapeDtypeStruct((nj, D_MODEL, PANEL), jnp.bfloat16),
        ],
        scratch_shapes=[pltpu.VMEM((tm + N_META, D_MODEL), jnp.bfloat16)],
        compiler_params=params(("arbitrary",)),
        name="in_proj_first",
    )(x2d, meta, g, w_f32)
    proj, w_out_b = pl.pallas_call(
        _in_proj_rest_kernel,
        grid=(ni, nj),
        in_specs=[
            pl.BlockSpec((tm, D_MODEL), lambda i, j: (i + 1, 0)),
            pl.BlockSpec((1, D_MODEL), lambda i, j: (0, 0)),
            pl.BlockSpec((None, D_MODEL, PANEL), lambda i, j: (j, 0, 0)),
            pl.BlockSpec(memory_space=pl.ANY),
            out_spec,
        ],
        out_specs=[pl.BlockSpec((None, tm, PANEL), lambda i, j: (j, i + 1, 0)), out_spec],
        out_shape=[jax.ShapeDtypeStruct((nj, rows, PANEL), jnp.bfloat16),
                   jax.ShapeDtypeStruct(w_out.shape, jnp.bfloat16)],
        scratch_shapes=[pltpu.VMEM((tm, D_MODEL), jnp.bfloat16)],
        input_output_aliases={3: 0},
        compiler_params=params(("arbitrary", "arbitrary")),
        name="in_proj_rest",
    )(x2d, g, w_bf16, proj, w_out)
    return proj, proj_meta, w_out_b


def _attn_kernel(q_ref, k_ref, v_ref, km_ref, vm_ref, bias_ref, g_ref, o_ref,
                 kmp_ref, vmp_ref):
    grp = pl.program_id(1)

    kmp_ref[...] = jnp.zeros_like(kmp_ref)
    vmp_ref[...] = jnp.zeros_like(vmp_ref)
    kmp_ref[0:N_META, :] = km_ref[...]
    vmp_ref[0:N_META, :] = vm_ref[...]

    row_head = lax.broadcasted_iota(jnp.int32, (GROUP_W, GROUP_W), 0) // HEAD_DIM
    col_head = lax.broadcasted_iota(jnp.int32, (GROUP_W, GROUP_W), 1) // HEAD_DIM
    diag = row_head == col_head
    lane = lax.broadcasted_iota(jnp.int32, (GRID_W, LANES), 1)
    low_half = lane < HEAD_DIM
    meta_mask = jnp.where(lane < N_META, 0.0, NEG_INF)
    scale = HEAD_DIM ** -0.5
    nt = (((1,), (1,)), ((), ()))

    def row_body(rl, carry):
        r = grp * ROWS_PER_STEP + rl
        rs = jnp.clip(r - WIN_ROWS // 2, 0, GRID_W - WIN_ROWS)
        case = r - rs
        q0 = pl.multiple_of(rl * GRID_W, GRID_W)
        k0 = pl.multiple_of(rs * GRID_W, GRID_W)
        outs = []
        for hg in range(N_GROUPS):
            lanes = slice(hg * GROUP_W, (hg + 1) * GROUP_W)
            qg = q_ref[pl.ds(q0, GRID_W), lanes] * scale
            qbd = jnp.where(diag, jnp.concatenate([qg] * HEADS_PER_GROUP, axis=0),
                            jnp.zeros((), jnp.bfloat16))
            kx = jnp.concatenate(
                [k_ref[pl.ds(k0, WIN_KEYS), lanes], kmp_ref[:, lanes]], axis=0)
            vx = jnp.concatenate(
                [v_ref[pl.ds(k0, WIN_KEYS), lanes], vmp_ref[:, lanes]], axis=0)
            s = lax.dot_general(qbd, kx, nt, preferred_element_type=jnp.float32)
            bias = jnp.concatenate([
                jnp.concatenate(
                    [bias_ref[hg * HEADS_PER_GROUP + h, 2 * p + (WIN_ROWS - 1) - case]
                     for p in range(WIN_ROWS // 2)] + [meta_mask], axis=1)
                for h in range(HEADS_PER_GROUP)], axis=0)
            s = s + bias
            p = jnp.exp(s - jnp.max(s, axis=-1, keepdims=True))
            inv_l = 1.0 / jnp.sum(p, axis=-1, keepdims=True)
            o_all = jnp.dot(p.astype(jnp.bfloat16), vx,
                            preferred_element_type=jnp.float32)
            tiles = []
            for t in range(GROUP_W // LANES):
                ra = slice(2 * t * GRID_W, (2 * t + 1) * GRID_W)
                rb = slice((2 * t + 1) * GRID_W, (2 * t + 2) * GRID_W)
                lt = slice(t * LANES, (t + 1) * LANES)
                tiles.append(jnp.where(low_half, o_all[ra, lt] * inv_l[ra],
                                       o_all[rb, lt] * inv_l[rb]))
            outs.append(jnp.concatenate(tiles, axis=1))
        ssq = sum(jnp.sum(o * o, axis=-1, keepdims=True) for o in outs)
        inv = lax.rsqrt(ssq / ATTN_DIM + RMS_EPS)
        for hg in range(N_GROUPS):
            lanes = slice(hg * GROUP_W, (hg + 1) * GROUP_W)
            o_ref[pl.ds(q0, GRID_W), lanes] = (
                outs[hg] * inv * g_ref[:, lanes]).astype(o_ref.dtype)
        return carry

    lax.fori_loop(0, ROWS_PER_STEP, row_body, 0, unroll=ROW_UNROLL)


def _attention(proj, proj_meta, bias, g, batch, seq):
    rows = seq // GRID_W
    steps = rows // ROWS_PER_STEP
    tq = ROWS_PER_STEP * GRID_W
    single = pl.Buffered(1)
    return pl.pallas_call(
        _attn_kernel,
        grid=(batch, steps),
        in_specs=[
            pl.BlockSpec((None, tq, PANEL), lambda b, s: (P_Q, b * steps + s, 0)),
            pl.BlockSpec((None, seq, PANEL), lambda b, s: (P_K, b, 0), pipeline_mode=single),
            pl.BlockSpec((None, seq, PANEL), lambda b, s: (P_V, b, 0), pipeline_mode=single),
            pl.BlockSpec((None, N_META, PANEL), lambda b, s: (P_K, 0, 0)),
            pl.BlockSpec((None, N_META, PANEL), lambda b, s: (P_V, 0, 0)),
            pl.BlockSpec(bias.shape, lambda b, s: (0, 0, 0, 0), pipeline_mode=single),
            pl.BlockSpec((1, ATTN_DIM), lambda b, s: (0, 0)),
        ],
        out_specs=pl.BlockSpec((tq, ATTN_DIM), lambda b, s: (b * steps + s, 0)),
        out_shape=jax.ShapeDtypeStruct((batch * seq, ATTN_DIM), jnp.bfloat16),
        scratch_shapes=[pltpu.VMEM((META_PAD, PANEL), jnp.bfloat16),
                        pltpu.VMEM((META_PAD, PANEL), jnp.bfloat16)],
        compiler_params=pltpu.CompilerParams(
            dimension_semantics=("arbitrary", "arbitrary"),
            vmem_limit_bytes=V7X_VMEM_LIMIT_BYTES),
        name="attention",
    )(proj, proj, proj, proj_meta, proj_meta, bias, g)


N_DR = 2 * WIN_ROWS - 1
N_DC = 2 * WIN_COLS - 1


N_PAIR = N_DR - 1


def _bias_kernel(rpb_ref, o_ref):
    h = pl.program_id(0)
    shape = (GRID_W, LANES)
    qc = lax.broadcasted_iota(jnp.int32, shape, 0)
    lane = lax.broadcasted_iota(jnp.int32, shape, 1)
    kc = lane % GRID_W
    dc = kc - qc + (WIN_COLS - 1)
    c0 = jnp.clip(qc - WIN_COLS // 2, 0, GRID_W - WIN_COLS)
    valid = (kc >= c0) & (kc < c0 + WIN_COLS)
    tiles = []
    for dr in range(N_DR):
        base = (h * N_DR + dr) * N_DC
        t = jnp.zeros(shape, jnp.float32)
        for j in range(N_DC):
            t = jnp.where(dc == j, rpb_ref[base + j], t)
        tiles.append(jnp.where(valid, t, NEG_INF))
    low_half = lane < GRID_W
    for dr0 in range(N_PAIR):
        o_ref[0, dr0] = jnp.where(low_half, tiles[dr0], tiles[dr0 + 1])


def _bias_table(rpb):
    return pl.pallas_call(
        _bias_kernel,
        grid=(ATTN_HEADS,),
        in_specs=[pl.BlockSpec(memory_space=pltpu.SMEM)],
        out_specs=pl.BlockSpec((1, N_PAIR, GRID_W, LANES), lambda h: (h, 0, 0, 0)),
        out_shape=jax.ShapeDtypeStruct((ATTN_HEADS, N_PAIR, GRID_W, LANES), jnp.float32),
        compiler_params=pltpu.CompilerParams(dimension_semantics=("arbitrary",)),
        name="bias_table",
    )(rpb.reshape(-1))


def _mix_kernel(b_ref, c_ref, u_ref, cp_ref, up_ref, cn_ref, un_ref, cm_ref, um_ref,
                ya_ref, x_ref, cw_ref, cb_ref, cg_ref, wo_ref, o_ref, *, blocks_per_seq):
    i = pl.program_id(0)
    tm = b_ref.shape[0]
    f32 = jnp.float32
    v = c_ref[...].astype(f32) * u_ref[...].astype(f32)
    last = BF16_SUBLANES - 1
    v_before = cp_ref[last:, :].astype(f32) * up_ref[last:, :].astype(f32)
    v_meta = cm_ref[N_META - 1:, :].astype(f32) * um_ref[N_META - 1:, :].astype(f32)
    v_after = cn_ref[0:1, :].astype(f32) * un_ref[0:1, :].astype(f32)
    seq_first = (i % blocks_per_seq) == 0
    seq_last = (i % blocks_per_seq) == blocks_per_seq - 1
    v_before = jnp.where(seq_first, v_meta, v_before)
    v_after = jnp.where(seq_last, 0.0, v_after)
    t = lax.broadcasted_iota(jnp.int32, (tm, 1), 0)
    v_prev = jnp.where(t == 0, v_before, pltpu.roll(v, 1, axis=0))
    v_next = jnp.where(t == tm - 1, v_after, pltpu.roll(v, tm - 1, axis=0))
    conv = (v_prev * cw_ref[0:1, :] + v * cw_ref[1:2, :] + v_next * cw_ref[2:3, :]
            + cb_ref[...])
    y = b_ref[...].astype(f32) * conv
    yc = _rms(y, cg_ref[...]).astype(jnp.bfloat16)
    acc = jnp.dot(yc, wo_ref[0:CONV_DIM, :], preferred_element_type=f32)
    acc = acc + jnp.dot(ya_ref[...], wo_ref[CONV_DIM:, :], preferred_element_type=f32)
    o_ref[...] = x_ref[...] + acc


def _mix(proj, proj_meta, y_attn, x2d, conv_w, conv_b, conv_g, w_out_bf16, seq, tm):
    rows = x2d.shape[0]
    nblk = rows // tm
    hb = tm // BF16_SUBLANES
    n_hb = rows // BF16_SUBLANES
    panel = lambda p: pl.BlockSpec((None, tm, PANEL), lambda i, p=p: (p, i, 0))
    prev = lambda p: pl.BlockSpec((None, BF16_SUBLANES, PANEL),
                                  lambda i, p=p: (p, jnp.maximum(i * hb - 1, 0), 0))
    nxt = lambda p: pl.BlockSpec((None, BF16_SUBLANES, PANEL),
                                 lambda i, p=p: (p, jnp.minimum((i + 1) * hb, n_hb - 1), 0))
    meta = lambda p: pl.BlockSpec((None, N_META, PANEL), lambda i, p=p: (p, 0, 0))
    vec = lambda n: pl.BlockSpec((n, CONV_DIM), lambda i: (0, 0))
    return pl.pallas_call(
        functools.partial(_mix_kernel, blocks_per_seq=seq // tm),
        grid=(nblk,),
        in_specs=[
            panel(P_B), panel(P_C), panel(P_U),
            prev(P_C), prev(P_U), nxt(P_C), nxt(P_U), meta(P_C), meta(P_U),
            pl.BlockSpec((tm, ATTN_DIM), lambda i: (i, 0)),
            pl.BlockSpec((tm, D_MODEL), lambda i: (i, 0)),
            vec(3), vec(1), vec(1),
            pl.BlockSpec((D_MODEL, D_MODEL), lambda i: (0, 0),
                         pipeline_mode=pl.Buffered(1)),
        ],
        out_specs=pl.BlockSpec((tm, D_MODEL), lambda i: (i, 0)),
        out_shape=jax.ShapeDtypeStruct((rows, D_MODEL), jnp.float32),
        compiler_params=pltpu.CompilerParams(
            dimension_semantics=("arbitrary",),
            vmem_limit_bytes=V7X_VMEM_LIMIT_BYTES),
        name="mix",
    )(proj, proj, proj, proj, proj, proj, proj, proj_meta, proj_meta,
      y_attn, x2d, conv_w, conv_b, conv_g, w_out_bf16)


def _mlp_step(f, nf, h_ref, g2_ref, wu_ref, wd_ref, gf_ref, o_ref, hn_ref):
    @pl.when(f == 0)
    def _():
        h = h_ref[...]
        hn_ref[...] = _rms(h, g2_ref[...]).astype(jnp.bfloat16)
        o_ref[...] = h

    wu = jnp.concatenate([wu_ref[s] for s in range(wu_ref.shape[0])], axis=1)
    a = jnp.dot(hn_ref[...], wu, preferred_element_type=jnp.float32)
    a = jnp.square(jnp.maximum(a, 0.0)).astype(jnp.bfloat16)
    o_ref[...] += jnp.dot(a, wd_ref[...], preferred_element_type=jnp.float32)

    @pl.when(f == nf - 1)
    def _():
        o_ref[...] = _rms(o_ref[...], gf_ref[...])


def _mlp_first_kernel(h_ref, g2_ref, wu_ref, wd_ref, gf_ref, o_ref, wub_ref, wdb_ref, hn_ref):
    slab_w = wub_ref.shape[2]
    for s in range(wub_ref.shape[0]):
        wub_ref[s] = wu_ref[:, s * slab_w:(s + 1) * slab_w].astype(jnp.bfloat16)
    wdb_ref[...] = wd_ref[...].astype(jnp.bfloat16)
    _mlp_step(pl.program_id(0), pl.num_programs(0),
              h_ref, g2_ref, wub_ref, wdb_ref, gf_ref, o_ref, hn_ref)


def _mlp_rest_kernel(h_ref, g2_ref, wu_ref, wd_ref, gf_ref, out_hbm_ref, o_ref, hn_ref):
    del out_hbm_ref
    _mlp_step(pl.program_id(1), pl.num_programs(1),
              h_ref, g2_ref, wu_ref, wd_ref, gf_ref, o_ref, hn_ref)


UP_SLAB = 256


def _mlp(h1, g2, w_up_f32, w_down_f32, gf, tm, tf):
    rows = h1.shape[0]
    n_slab = tf // UP_SLAB
    params = lambda sem: pltpu.CompilerParams(
        dimension_semantics=sem, vmem_limit_bytes=V7X_VMEM_LIMIT_BYTES)
    once = pl.Buffered(1)
    out, w_up_b, w_down_b = pl.pallas_call(
        _mlp_first_kernel,
        grid=(D_FF // tf,),
        in_specs=[
            pl.BlockSpec((tm, D_MODEL), lambda f: (0, 0), pipeline_mode=once),
            pl.BlockSpec((1, D_MODEL), lambda f: (0, 0)),
            pl.BlockSpec((D_MODEL, tf), lambda f: (0, f)),
            pl.BlockSpec((tf, D_MODEL), lambda f: (f, 0)),
            pl.BlockSpec((1, D_MODEL), lambda f: (0, 0)),
        ],
        out_specs=[
            pl.BlockSpec((tm, D_MODEL), lambda f: (0, 0), pipeline_mode=once),
            pl.BlockSpec((n_slab, D_MODEL, UP_SLAB), lambda f: (f, 0, 0)),
            pl.BlockSpec((tf, D_MODEL), lambda f: (f, 0)),
        ],
        out_shape=[
            jax.ShapeDtypeStruct((rows, D_MODEL), jnp.float32),
            jax.ShapeDtypeStruct((D_FF // UP_SLAB, D_MODEL, UP_SLAB), jnp.bfloat16),
            jax.ShapeDtypeStruct((D_FF, D_MODEL), jnp.bfloat16),
        ],
        scratch_shapes=[pltpu.VMEM((tm, D_MODEL), jnp.bfloat16)],
        compiler_params=params(("arbitrary",)),
        name="mlp_first",
    )(h1, g2, w_up_f32, w_down_f32, gf)
    return pl.pallas_call(
        _mlp_rest_kernel,
        grid=(rows // tm - 1, D_FF // tf),
        in_specs=[
            pl.BlockSpec((tm, D_MODEL), lambda i, f: (i + 1, 0)),
            pl.BlockSpec((1, D_MODEL), lambda i, f: (0, 0)),
            pl.BlockSpec((n_slab, D_MODEL, UP_SLAB), lambda i, f: (f, 0, 0)),
            pl.BlockSpec((tf, D_MODEL), lambda i, f: (f, 0)),
            pl.BlockSpec((1, D_MODEL), lambda i, f: (0, 0)),
            pl.BlockSpec(memory_space=pl.ANY),
        ],
        out_specs=pl.BlockSpec((tm, D_MODEL), lambda i, f: (i + 1, 0)),
        out_shape=jax.ShapeDtypeStruct((rows, D_MODEL), jnp.float32),
        scratch_shapes=[pltpu.VMEM((tm, D_MODEL), jnp.bfloat16)],
        input_output_aliases={5: 0},
        compiler_params=params(("arbitrary", "arbitrary")),
        name="mlp_rest",
    )(h1, g2, w_up_b, w_down_b, gf, out)


def kernel(x, meta_tokens, norm1_g, w_in, conv_w, conv_b, conv_norm_g, attn_rpb,
           attn_norm_g, w_out, norm2_g, w_up, w_down, final_norm_g):
    batch, seq, _ = x.shape
    x2d = x.reshape(batch * seq, D_MODEL)
    proj, proj_meta, w_out_b = _in_proj(x2d, meta_tokens, norm1_g, w_in[0], w_out[0], tm=1024)
    bias = _bias_table(attn_rpb[0])
    y_attn = _attention(proj, proj_meta, bias, attn_norm_g, batch, seq)
    h1 = _mix(proj, proj_meta, y_attn, x2d, conv_w[0], conv_b, conv_norm_g, w_out_b,
              seq, tm=512)
    out = _mlp(h1, norm2_g, w_up[0], w_down[0], final_norm_g.reshape(1, D_MODEL),
               tm=1024, tf=512)
    return out.reshape(batch, seq, D_MODEL)
```

```python
import functools

import jax
import jax.numpy as jnp
from jax import lax
from jax.experimental import pallas as pl
from jax.experimental.pallas import tpu as pltpu

D_MODEL = 2048
N_META = 16
CONV_DIM = D_MODEL // 2
ATTN_HEADS = 16
HEAD_DIM = 64
ATTN_DIM = ATTN_HEADS * HEAD_DIM
PROJ_DIM = 3 * CONV_DIM + 3 * ATTN_DIM
D_FF = 4 * D_MODEL
GRID_W = 64
WIN_ROWS = 8
WIN_COLS = 16
RMS_EPS = 1e-6
NEG_INF = -1e30

PANEL = 1024
P_B, P_C, P_U, P_Q, P_K, P_V = range(6)

V7X_VMEM_LIMIT_BYTES = 56 * 1024 * 1024
BF16_SUBLANES = 16
LANES = 128

HEADS_PER_GROUP = 4
GROUP_W = HEADS_PER_GROUP * HEAD_DIM
N_GROUPS = ATTN_HEADS // HEADS_PER_GROUP
ROWS_PER_STEP = 8
WIN_KEYS = WIN_ROWS * GRID_W
META_PAD = LANES
ALL_KEYS = WIN_KEYS + META_PAD
ROW_UNROLL = 8


def _rms(xf, g):
    return xf * lax.rsqrt(jnp.mean(xf * xf, axis=-1, keepdims=True) + RMS_EPS) * g


def _in_proj_first_kernel(x_ref, meta_ref, g_ref, w_ref, o_ref, om_ref, wb_ref, hn_ref):
    tm = x_ref.shape[0]

    @pl.when(pl.program_id(0) == 0)
    def _():
        hn_ref[0:tm, :] = _rms(x_ref[...], g_ref[...]).astype(jnp.bfloat16)
        hn_ref[tm:, :] = _rms(meta_ref[...], g_ref[...]).astype(jnp.bfloat16)

    w = w_ref[...].astype(jnp.bfloat16)
    wb_ref[...] = w
    r = jnp.dot(hn_ref[...], w, preferred_element_type=jnp.float32)
    o_ref[...] = r[0:tm].astype(o_ref.dtype)
    om_ref[...] = r[tm:].astype(om_ref.dtype)


def _in_proj_rest_kernel(x_ref, g_ref, w_ref, first_ref, wo_ref, o_ref, wob_ref, hn_ref):
    i = pl.program_id(0)

    @pl.when(i == 0)
    def _():
        o_ref[...] = first_ref[...]

    @pl.when(i > 0)
    def _():
        @pl.when(pl.program_id(1) == 0)
        def _():
            hn_ref[...] = _rms(x_ref[...], g_ref[...]).astype(jnp.bfloat16)

        o_ref[...] = jnp.dot(hn_ref[...], w_ref[...],
                             preferred_element_type=jnp.float32).astype(o_ref.dtype)

    wob_ref[...] = wo_ref[...].astype(jnp.bfloat16)


CAST_STEPS = 32


def _in_proj(x2d, meta, g, w_f32, w_out, tm):
    rows = x2d.shape[0]
    nj = PROJ_DIM // PANEL
    ni = rows // tm
    assert ni * nj >= CAST_STEPS
    slab = lambda i, j: jnp.minimum(i * nj + j, CAST_STEPS - 1)
    out_spec = pl.BlockSpec((D_MODEL // CAST_STEPS, D_MODEL), lambda i, j: (slab(i, j), 0))
    params = lambda sem: pltpu.CompilerParams(
        dimension_semantics=sem, vmem_limit_bytes=V7X_VMEM_LIMIT_BYTES)
    proj0, proj_meta, w_bf16 = pl.pallas_call(
        _in_proj_first_kernel,
        grid=(nj,),
        in_specs=[
            pl.BlockSpec((tm, D_MODEL), lambda j: (0, 0)),
            pl.BlockSpec((N_META, D_MODEL), lambda j: (0, 0)),
            pl.BlockSpec((1, D_MODEL), lambda j: (0, 0)),
            pl.BlockSpec((D_MODEL, PANEL), lambda j: (0, j)),
        ],
        out_specs=[
            pl.BlockSpec((None, tm, PANEL), lambda j: (j, 0, 0)),
            pl.BlockSpec((None, N_META, PANEL), lambda j: (j, 0, 0)),
            pl.BlockSpec((None, D_MODEL, PANEL), lambda j: (j, 0, 0)),
        ],
        out_shape=[
            jax.ShapeDtypeStruct((nj, tm, PANEL), jnp.bfloat16),
            jax.ShapeDtypeStruct((nj, N_META, PANEL), jnp.bfloat16),
            jax.ShapeDtypeStruct((nj, D_MODEL, PANEL), jnp.bfloat16),
        ],
        scratch_shapes=[pltpu.VMEM((tm + N_META, D_MODEL), jnp.bfloat16)],
        compiler_params=params(("arbitrary",)),
        name="in_proj_first",
    )(x2d, meta, g, w_f32)
    proj, w_out_b = pl.pallas_call(
        _in_proj_rest_kernel,
        grid=(ni, nj),
        in_specs=[
            pl.BlockSpec((tm, D_MODEL), lambda i, j: (jnp.maximum(i, 1), 0)),
            pl.BlockSpec((1, D_MODEL), lambda i, j: (0, 0)),
            pl.BlockSpec((None, D_MODEL, PANEL), lambda i, j: (j, 0, 0)),
            pl.BlockSpec((None, tm, PANEL), lambda i, j: (jnp.where(i == 0, j, nj - 1), 0, 0)),
            out_spec,
        ],
        out_specs=[pl.BlockSpec((None, tm, PANEL), lambda i, j: (j, i, 0)), out_spec],
        out_shape=[jax.ShapeDtypeStruct((nj, rows, PANEL), jnp.bfloat16),
                   jax.ShapeDtypeStruct(w_out.shape, jnp.bfloat16)],
        scratch_shapes=[pltpu.VMEM((tm, D_MODEL), jnp.bfloat16)],
        compiler_params=params(("arbitrary", "arbitrary")),
        name="in_proj_rest",
    )(x2d, g, w_bf16, proj0, w_out)
    return proj, proj_meta, w_out_b


def _attn_kernel(q_ref, k_ref, v_ref, km_ref, vm_ref, bias_ref, g_ref, o_ref,
                 kmp_ref, vmp_ref):
    grp = pl.program_id(1)

    kmp_ref[...] = jnp.zeros_like(kmp_ref)
    vmp_ref[...] = jnp.zeros_like(vmp_ref)
    kmp_ref[0:N_META, :] = km_ref[...]
    vmp_ref[0:N_META, :] = vm_ref[...]

    row_head = lax.broadcasted_iota(jnp.int32, (GROUP_W, GROUP_W), 0) // HEAD_DIM
    col_head = lax.broadcasted_iota(jnp.int32, (GROUP_W, GROUP_W), 1) // HEAD_DIM
    diag = row_head == col_head
    lane = lax.broadcasted_iota(jnp.int32, (GRID_W, LANES), 1)
    low_half = lane < HEAD_DIM
    meta_mask = jnp.where(lane < N_META, 0.0, NEG_INF)
    scale = HEAD_DIM ** -0.5
    nt = (((1,), (1,)), ((), ()))

    def row_body(rl, carry):
        r = grp * ROWS_PER_STEP + rl
        rs = jnp.clip(r - WIN_ROWS // 2, 0, GRID_W - WIN_ROWS)
        case = r - rs
        q0 = pl.multiple_of(rl * GRID_W, GRID_W)
        k0 = pl.multiple_of(rs * GRID_W, GRID_W)
        outs = []
        for hg in range(N_GROUPS):
            lanes = slice(hg * GROUP_W, (hg + 1) * GROUP_W)
            qg = q_ref[pl.ds(q0, GRID_W), lanes] * scale
            qbd = jnp.where(diag, jnp.concatenate([qg] * HEADS_PER_GROUP, axis=0),
                            jnp.zeros((), jnp.bfloat16))
            kx = jnp.concatenate(
                [k_ref[pl.ds(k0, WIN_KEYS), lanes], kmp_ref[:, lanes]], axis=0)
            vx = jnp.concatenate(
                [v_ref[pl.ds(k0, WIN_KEYS), lanes], vmp_ref[:, lanes]], axis=0)
            s = lax.dot_general(qbd, kx, nt, preferred_element_type=jnp.float32)
            bias = jnp.concatenate([
                jnp.concatenate(
                    [bias_ref[hg * HEADS_PER_GROUP + h, 2 * p + (WIN_ROWS - 1) - case]
                     for p in range(WIN_ROWS // 2)] + [meta_mask], axis=1)
                for h in range(HEADS_PER_GROUP)], axis=0)
            s = s + bias
            p = jnp.exp(s - jnp.max(s, axis=-1, keepdims=True))
            inv_l = 1.0 / jnp.sum(p, axis=-1, keepdims=True)
            o_all = jnp.dot(p.astype(jnp.bfloat16), vx,
                            preferred_element_type=jnp.float32)
            tiles = []
            for t in range(GROUP_W // LANES):
                ra = slice(2 * t * GRID_W, (2 * t + 1) * GRID_W)
                rb = slice((2 * t + 1) * GRID_W, (2 * t + 2) * GRID_W)
                lt = slice(t * LANES, (t + 1) * LANES)
                tiles.append(jnp.where(low_half, o_all[ra, lt] * inv_l[ra],
                                       o_all[rb, lt] * inv_l[rb]))
            outs.append(jnp.concatenate(tiles, axis=1))
        ssq = sum(jnp.sum(o * o, axis=-1, keepdims=True) for o in outs)
        inv = lax.rsqrt(ssq / ATTN_DIM + RMS_EPS)
        for hg in range(N_GROUPS):
            lanes = slice(hg * GROUP_W, (hg + 1) * GROUP_W)
            o_ref[pl.ds(q0, GRID_W), lanes] = (
                outs[hg] * inv * g_ref[:, lanes]).astype(o_ref.dtype)
        return carry

    lax.fori_loop(0, ROWS_PER_STEP, row_body, 0, unroll=ROW_UNROLL)


def _attention(proj, proj_meta, bias, g, batch, seq):
    rows = seq // GRID_W
    steps = rows // ROWS_PER_STEP
    tq = ROWS_PER_STEP * GRID_W
    single = pl.Buffered(1)
    return pl.pallas_call(
        _attn_kernel,
        grid=(batch, steps),
        in_specs=[
            pl.BlockSpec((None, tq, PANEL), lambda b, s: (P_Q, b * steps + s, 0)),
            pl.BlockSpec((None, seq, PANEL), lambda b, s: (P_K, b, 0), pipeline_mode=single),
            pl.BlockSpec((None, seq, PANEL), lambda b, s: (P_V, b, 0), pipeline_mode=single),
            pl.BlockSpec((None, N_META, PANEL), lambda b, s: (P_K, 0, 0)),
            pl.BlockSpec((None, N_META, PANEL), lambda b, s: (P_V, 0, 0)),
            pl.BlockSpec(bias.shape, lambda b, s: (0, 0, 0, 0), pipeline_mode=single),
            pl.BlockSpec((1, ATTN_DIM), lambda b, s: (0, 0)),
        ],
        out_specs=pl.BlockSpec((tq, ATTN_DIM), lambda b, s: (b * steps + s, 0)),
        out_shape=jax.ShapeDtypeStruct((batch * seq, ATTN_DIM), jnp.bfloat16),
        scratch_shapes=[pltpu.VMEM((META_PAD, PANEL), jnp.bfloat16),
                        pltpu.VMEM((META_PAD, PANEL), jnp.bfloat16)],
        compiler_params=pltpu.CompilerParams(
            dimension_semantics=("arbitrary", "arbitrary"),
            vmem_limit_bytes=V7X_VMEM_LIMIT_BYTES),
        name="attention",
    )(proj, proj, proj, proj_meta, proj_meta, bias, g)


N_DR = 2 * WIN_ROWS - 1
N_DC = 2 * WIN_COLS - 1


N_PAIR = N_DR - 1


def _bias_kernel(rpb_ref, o_ref):
    h = pl.program_id(0)
    shape = (GRID_W, LANES)
    qc = lax.broadcasted_iota(jnp.int32, shape, 0)
    lane = lax.broadcasted_iota(jnp.int32, shape, 1)
    kc = lane % GRID_W
    dc = kc - qc + (WIN_COLS - 1)
    c0 = jnp.clip(qc - WIN_COLS // 2, 0, GRID_W - WIN_COLS)
    valid = (kc >= c0) & (kc < c0 + WIN_COLS)
    tiles = []
    for dr in range(N_DR):
        base = (h * N_DR + dr) * N_DC
        t = jnp.zeros(shape, jnp.float32)
        for j in range(N_DC):
            t = jnp.where(dc == j, rpb_ref[base + j], t)
        tiles.append(jnp.where(valid, t, NEG_INF))
    low_half = lane < GRID_W
    for dr0 in range(N_PAIR):
        o_ref[0, dr0] = jnp.where(low_half, tiles[dr0], tiles[dr0 + 1])


def _bias_table(rpb):
    return pl.pallas_call(
        _bias_kernel,
        grid=(ATTN_HEADS,),
        in_specs=[pl.BlockSpec(memory_space=pltpu.SMEM)],
        out_specs=pl.BlockSpec((1, N_PAIR, GRID_W, LANES), lambda h: (h, 0, 0, 0)),
        out_shape=jax.ShapeDtypeStruct((ATTN_HEADS, N_PAIR, GRID_W, LANES), jnp.float32),
        compiler_params=pltpu.CompilerParams(dimension_semantics=("arbitrary",)),
        name="bias_table",
    )(rpb.reshape(-1))


def _mix_kernel(b_ref, c_ref, u_ref, cp_ref, up_ref, cn_ref, un_ref, cm_ref, um_ref,
                ya_ref, x_ref, cw_ref, cb_ref, cg_ref, wo_ref, o_ref, *, blocks_per_seq):
    i = pl.program_id(0)
    tm = b_ref.shape[0]
    f32 = jnp.float32
    v = c_ref[...].astype(f32) * u_ref[...].astype(f32)
    last = BF16_SUBLANES - 1
    v_before = cp_ref[last:, :].astype(f32) * up_ref[last:, :].astype(f32)
    v_meta = cm_ref[N_META - 1:, :].astype(f32) * um_ref[N_META - 1:, :].astype(f32)
    v_after = cn_ref[0:1, :].astype(f32) * un_ref[0:1, :].astype(f32)
    seq_first = (i % blocks_per_seq) == 0
    seq_last = (i % blocks_per_seq) == blocks_per_seq - 1
    v_before = jnp.where(seq_first, v_meta, v_before)
    v_after = jnp.where(seq_last, 0.0, v_after)
    t = lax.broadcasted_iota(jnp.int32, (tm, 1), 0)
    v_prev = jnp.where(t == 0, v_before, pltpu.roll(v, 1, axis=0))
    v_next = jnp.where(t == tm - 1, v_after, pltpu.roll(v, tm - 1, axis=0))
    conv = (v_prev * cw_ref[0:1, :] + v * cw_ref[1:2, :] + v_next * cw_ref[2:3, :]
            + cb_ref[...])
    y = b_ref[...].astype(f32) * conv
    yc = _rms(y, cg_ref[...]).astype(jnp.bfloat16)
    acc = jnp.dot(yc, wo_ref[0:CONV_DIM, :], preferred_element_type=f32)
    acc = acc + jnp.dot(ya_ref[...], wo_ref[CONV_DIM:, :], preferred_element_type=f32)
    o_ref[...] = x_ref[...] + acc


def _mix(proj, proj_meta, y_attn, x2d, conv_w, conv_b, conv_g, w_out_bf16, seq, tm):
    rows = x2d.shape[0]
    nblk = rows // tm
    hb = tm // BF16_SUBLANES
    n_hb = rows // BF16_SUBLANES
    panel = lambda p: pl.BlockSpec((None, tm, PANEL), lambda i, p=p: (p, i, 0))
    prev = lambda p: pl.BlockSpec((None, BF16_SUBLANES, PANEL),
                                  lambda i, p=p: (p, jnp.maximum(i * hb - 1, 0), 0))
    nxt = lambda p: pl.BlockSpec((None, BF16_SUBLANES, PANEL),
                                 lambda i, p=p: (p, jnp.minimum((i + 1) * hb, n_hb - 1), 0))
    meta = lambda p: pl.BlockSpec((None, N_META, PANEL), lambda i, p=p: (p, 0, 0))
    vec = lambda n: pl.BlockSpec((n, CONV_DIM), lambda i: (0, 0))
    return pl.pallas_call(
        functools.partial(_mix_kernel, blocks_per_seq=seq // tm),
        grid=(nblk,),
        in_specs=[
            panel(P_B), panel(P_C), panel(P_U),
            prev(P_C), prev(P_U), nxt(P_C), nxt(P_U), meta(P_C), meta(P_U),
            pl.BlockSpec((tm, ATTN_DIM), lambda i: (i, 0)),
            pl.BlockSpec((tm, D_MODEL), lambda i: (i, 0)),
            vec(3), vec(1), vec(1),
            pl.BlockSpec((D_MODEL, D_MODEL), lambda i: (0, 0),
                         pipeline_mode=pl.Buffered(1)),
        ],
        out_specs=pl.BlockSpec((tm, D_MODEL), lambda i: (i, 0)),
        out_shape=jax.ShapeDtypeStruct((rows, D_MODEL), jnp.float32),
        compiler_params=pltpu.CompilerParams(
            dimension_semantics=("arbitrary",),
            vmem_limit_bytes=V7X_VMEM_LIMIT_BYTES),
        name="mix",
    )(proj, proj, proj, proj, proj, proj, proj, proj_meta, proj_meta,
      y_attn, x2d, conv_w, conv_b, conv_g, w_out_bf16)


def _mlp_step(f, nf, h_ref, g2_ref, wu_ref, wd_ref, gf_ref, o_ref, hn_ref):
    @pl.when(f == 0)
    def _():
        h = h_ref[...]
        hn_ref[...] = _rms(h, g2_ref[...]).astype(jnp.bfloat16)
        o_ref[...] = h

    wu = jnp.concatenate([wu_ref[s] for s in range(wu_ref.shape[0])], axis=1)
    a = jnp.dot(hn_ref[...], wu, preferred_element_type=jnp.float32)
    a = jnp.square(jnp.maximum(a, 0.0)).astype(jnp.bfloat16)
    o_ref[...] += jnp.dot(a, wd_ref[...], preferred_element_type=jnp.float32)

    @pl.when(f == nf - 1)
    def _():
        o_ref[...] = _rms(o_ref[...], gf_ref[...])


def _mlp_first_kernel(h_ref, g2_ref, wu_ref, wd_ref, gf_ref, o_ref, wub_ref, wdb_ref, hn_ref):
    slab_w = wub_ref.shape[2]
    for s in range(wub_ref.shape[0]):
        wub_ref[s] = wu_ref[:, s * slab_w:(s + 1) * slab_w].astype(jnp.bfloat16)
    wdb_ref[...] = wd_ref[...].astype(jnp.bfloat16)
    _mlp_step(pl.program_id(0), pl.num_programs(0),
              h_ref, g2_ref, wub_ref, wdb_ref, gf_ref, o_ref, hn_ref)


def _mlp_rest_kernel(h_ref, g2_ref, wu_ref, wd_ref, gf_ref, o_ref, hn_ref):
    _mlp_step(pl.program_id(1), pl.num_programs(1),
              h_ref, g2_ref, wu_ref, wd_ref, gf_ref, o_ref, hn_ref)


UP_SLAB = 256


def _mlp(h1, g2, w_up_f32, w_down_f32, gf, tm, tf):
    rows = h1.shape[0]
    n_slab = tf // UP_SLAB
    params = lambda sem: pltpu.CompilerParams(
        dimension_semantics=sem, vmem_limit_bytes=V7X_VMEM_LIMIT_BYTES)
    once = pl.Buffered(1)
    out, w_up_b, w_down_b = pl.pallas_call(
        _mlp_first_kernel,
        grid=(D_FF // tf,),
        in_specs=[
            pl.BlockSpec((tm, D_MODEL), lambda f: (0, 0), pipeline_mode=once),
            pl.BlockSpec((1, D_MODEL), lambda f: (0, 0)),
            pl.BlockSpec((D_MODEL, tf), lambda f: (0, f)),
            pl.BlockSpec((tf, D_MODEL), lambda f: (f, 0)),
            pl.BlockSpec((1, D_MODEL), lambda f: (0, 0)),
        ],
        out_specs=[
            pl.BlockSpec((tm, D_MODEL), lambda f: (0, 0), pipeline_mode=once),
            pl.BlockSpec((n_slab, D_MODEL, UP_SLAB), lambda f: (f, 0, 0)),
            pl.BlockSpec((tf, D_MODEL), lambda f: (f, 0)),
        ],
        out_shape=[
            jax.ShapeDtypeStruct((rows, D_MODEL), jnp.float32),
            jax.ShapeDtypeStruct((D_FF // UP_SLAB, D_MODEL, UP_SLAB), jnp.bfloat16),
            jax.ShapeDtypeStruct((D_FF, D_MODEL), jnp.bfloat16),
        ],
        scratch_shapes=[pltpu.VMEM((tm, D_MODEL), jnp.bfloat16)],
        input_output_aliases={0: 0},
        compiler_params=params(("arbitrary",)),
        name="mlp_first",
    )(h1, g2, w_up_f32, w_down_f32, gf)
    return pl.pallas_call(
        _mlp_rest_kernel,
        grid=(rows // tm - 1, D_FF // tf),
        in_specs=[
            pl.BlockSpec((tm, D_MODEL), lambda i, f: (i + 1, 0)),
            pl.BlockSpec((1, D_MODEL), lambda i, f: (0, 0)),
            pl.BlockSpec((n_slab, D_MODEL, UP_SLAB), lambda i, f: (f, 0, 0)),
            pl.BlockSpec((tf, D_MODEL), lambda i, f: (f, 0)),
            pl.BlockSpec((1, D_MODEL), lambda i, f: (0, 0)),
        ],
        out_specs=pl.BlockSpec((tm, D_MODEL), lambda i, f: (i + 1, 0)),
        out_shape=jax.ShapeDtypeStruct((rows, D_MODEL), jnp.float32),
        scratch_shapes=[pltpu.VMEM((tm, D_MODEL), jnp.bfloat16)],
        input_output_aliases={0: 0},
        compiler_params=params(("arbitrary", "arbitrary")),
        name="mlp_rest",
    )(out, g2, w_up_b, w_down_b, gf)


def kernel(x, meta_tokens, norm1_g, w_in, conv_w, conv_b, conv_norm_g, attn_rpb,
           attn_norm_g, w_out, norm2_g, w_up, w_down, final_norm_g):
    batch, seq, _ = x.shape
    x2d = x.reshape(batch * seq, D_MODEL)
    proj, proj_meta, w_out_b = _in_proj(x2d, meta_tokens, norm1_g, w_in[0], w_out[0], tm=1024)
    bias = _bias_table(attn_rpb[0])
    y_attn = _attention(proj, proj_meta, bias, attn_norm_g, batch, seq)
    h1 = _mix(proj, proj_meta, y_attn, x2d, conv_w[0], conv_b, conv_norm_g, w_out_b,
              seq, tm=512)
    out = _mlp(h1, norm2_g, w_up[0], w_down[0], final_norm_g.reshape(1, D_MODEL),
               tm=1024, tf=512)
    return out.reshape(batch, seq, D_MODEL)
```

```python
import functools

import jax
import jax.numpy as jnp
from jax import lax
from jax.experimental import pallas as pl
from jax.experimental.pallas import tpu as pltpu

D_MODEL = 2048
N_META = 16
CONV_DIM = D_MODEL // 2
ATTN_HEADS = 16
HEAD_DIM = 64
ATTN_DIM = ATTN_HEADS * HEAD_DIM
PROJ_DIM = 3 * CONV_DIM + 3 * ATTN_DIM
D_FF = 4 * D_MODEL
GRID_W = 64
WIN_ROWS = 8
WIN_COLS = 16
RMS_EPS = 1e-6
NEG_INF = -1e30

PANEL = 1024
P_B, P_C, P_U, P_Q, P_K, P_V = range(6)

V7X_VMEM_LIMIT_BYTES = 56 * 1024 * 1024
BF16_SUBLANES = 16
LANES = 128

HEADS_PER_GROUP = 4
GROUP_W = HEADS_PER_GROUP * HEAD_DIM
N_GROUPS = ATTN_HEADS // HEADS_PER_GROUP
ROWS_PER_STEP = 8
WIN_KEYS = WIN_ROWS * GRID_W
META_PAD = LANES
ALL_KEYS = WIN_KEYS + META_PAD
ROW_UNROLL = 8


def _rms(xf, g):
    return xf * lax.rsqrt(jnp.mean(xf * xf, axis=-1, keepdims=True) + RMS_EPS) * g


def _in_proj_first_kernel(x_ref, meta_ref, g_ref, w_ref, o_ref, om_ref, wb_ref, hn_ref):
    tm = x_ref.shape[0]

    @pl.when(pl.program_id(0) == 0)
    def _():
        hn_ref[0:tm, :] = _rms(x_ref[...], g_ref[...]).astype(jnp.bfloat16)
        hn_ref[tm:, :] = _rms(meta_ref[...], g_ref[...]).astype(jnp.bfloat16)

    w = w_ref[...].astype(jnp.bfloat16)
    wb_ref[...] = w
    r = jnp.dot(hn_ref[...], w, preferred_element_type=jnp.float32)
    o_ref[...] = r[0:tm].astype(o_ref.dtype)
    om_ref[...] = r[tm:].astype(om_ref.dtype)


def _in_proj_rest_kernel(x_ref, g_ref, w_ref, first_ref, wo_ref, o_ref, wob_ref, hn_ref):
    i = pl.program_id(0)

    @pl.when(i == 0)
    def _():
        o_ref[...] = first_ref[...]

    @pl.when(i > 0)
    def _():
        @pl.when(pl.program_id(1) == 0)
        def _():
            hn_ref[...] = _rms(x_ref[...], g_ref[...]).astype(jnp.bfloat16)

        o_ref[...] = jnp.dot(hn_ref[...], w_ref[...],
                             preferred_element_type=jnp.float32).astype(o_ref.dtype)

    wob_ref[...] = wo_ref[...].astype(jnp.bfloat16)


CAST_STEPS = 32


def _in_proj(x2d, meta, g, w_f32, w_out, tm):
    rows = x2d.shape[0]
    nj = PROJ_DIM // PANEL
    ni = rows // tm
    assert ni * nj >= CAST_STEPS
    slab = lambda i, j: jnp.minimum(i * nj + j, CAST_STEPS - 1)
    out_spec = pl.BlockSpec((D_MODEL // CAST_STEPS, D_MODEL), lambda i, j: (slab(i, j), 0))
    params = lambda sem: pltpu.CompilerParams(
        dimension_semantics=sem, vmem_limit_bytes=V7X_VMEM_LIMIT_BYTES)
    proj0, proj_meta, w_bf16 = pl.pallas_call(
        _in_proj_first_kernel,
        grid=(nj,),
        in_specs=[
            pl.BlockSpec((tm, D_MODEL), lambda j: (0, 0)),
            pl.BlockSpec((N_META, D_MODEL), lambda j: (0, 0)),
            pl.BlockSpec((1, D_MODEL), lambda j: (0, 0)),
            pl.BlockSpec((D_MODEL, PANEL), lambda j: (0, j)),
        ],
        out_specs=[
            pl.BlockSpec((None, tm, PANEL), lambda j: (j, 0, 0)),
            pl.BlockSpec((None, N_META, PANEL), lambda j: (j, 0, 0)),
            pl.BlockSpec((None, D_MODEL, PANEL), lambda j: (j, 0, 0)),
        ],
        out_shape=[
            jax.ShapeDtypeStruct((nj, tm, PANEL), jnp.bfloat16),
            jax.ShapeDtypeStruct((nj, N_META, PANEL), jnp.bfloat16),
            jax.ShapeDtypeStruct((nj, D_MODEL, PANEL), jnp.bfloat16),
        ],
        scratch_shapes=[pltpu.VMEM((tm + N_META, D_MODEL), jnp.bfloat16)],
        compiler_params=params(("arbitrary",)),
        name="in_proj_first",
    )(x2d, meta, g, w_f32)
    proj, w_out_b = pl.pallas_call(
        _in_proj_rest_kernel,
        grid=(ni, nj),
        in_specs=[
            pl.BlockSpec((tm, D_MODEL), lambda i, j: (jnp.maximum(i, 1), 0)),
            pl.BlockSpec((1, D_MODEL), lambda i, j: (0, 0)),
            pl.BlockSpec((None, D_MODEL, PANEL), lambda i, j: (jnp.where(i == 0, 0, j), 0, 0)),
            pl.BlockSpec((None, tm, PANEL), lambda i, j: (jnp.where(i == 0, j, nj - 1), 0, 0)),
            out_spec,
        ],
        out_specs=[pl.BlockSpec((None, tm, PANEL), lambda i, j: (j, i, 0)), out_spec],
        out_shape=[jax.ShapeDtypeStruct((nj, rows, PANEL), jnp.bfloat16),
                   jax.ShapeDtypeStruct(w_out.shape, jnp.bfloat16)],
        scratch_shapes=[pltpu.VMEM((tm, D_MODEL), jnp.bfloat16)],
        compiler_params=params(("arbitrary", "arbitrary")),
        name="in_proj_rest",
    )(x2d, g, w_bf16, proj0, w_out)
    return proj, proj_meta, w_out_b


def _attn_kernel(q_ref, k_ref, v_ref, km_ref, vm_ref, bias_ref, g_ref, o_ref,
                 kmp_ref, vmp_ref):
    grp = pl.program_id(1)

    kmp_ref[...] = jnp.zeros_like(kmp_ref)
    vmp_ref[...] = jnp.zeros_like(vmp_ref)
    kmp_ref[0:N_META, :] = km_ref[...]
    vmp_ref[0:N_META, :] = vm_ref[...]

    row_head = lax.broadcasted_iota(jnp.int32, (GROUP_W, GROUP_W), 0) // HEAD_DIM
    col_head = lax.broadcasted_iota(jnp.int32, (GROUP_W, GROUP_W), 1) // HEAD_DIM
    diag = row_head == col_head
    lane = lax.broadcasted_iota(jnp.int32, (GRID_W, LANES), 1)
    low_half = lane < HEAD_DIM
    meta_mask = jnp.where(lane < N_META, 0.0, NEG_INF)
    scale = HEAD_DIM ** -0.5
    nt = (((1,), (1,)), ((), ()))

    def row_body(rl, carry):
        r = grp * ROWS_PER_STEP + rl
        rs = jnp.clip(r - WIN_ROWS // 2, 0, GRID_W - WIN_ROWS)
        case = r - rs
        q0 = pl.multiple_of(rl * GRID_W, GRID_W)
        k0 = pl.multiple_of(rs * GRID_W, GRID_W)
        outs = []
        for hg in range(N_GROUPS):
            lanes = slice(hg * GROUP_W, (hg + 1) * GROUP_W)
            qg = q_ref[pl.ds(q0, GRID_W), lanes] * scale
            qbd = jnp.where(diag, jnp.concatenate([qg] * HEADS_PER_GROUP, axis=0),
                            jnp.zeros((), jnp.bfloat16))
            kx = jnp.concatenate(
                [k_ref[pl.ds(k0, WIN_KEYS), lanes], kmp_ref[:, lanes]], axis=0)
            vx = jnp.concatenate(
                [v_ref[pl.ds(k0, WIN_KEYS), lanes], vmp_ref[:, lanes]], axis=0)
            s = lax.dot_general(qbd, kx, nt, preferred_element_type=jnp.float32)
            bias = jnp.concatenate([
                jnp.concatenate(
                    [bias_ref[hg * HEADS_PER_GROUP + h, 2 * p + (WIN_ROWS - 1) - case]
                     for p in range(WIN_ROWS // 2)] + [meta_mask], axis=1)
                for h in range(HEADS_PER_GROUP)], axis=0)
            s = s + bias
            p = jnp.exp(s - jnp.max(s, axis=-1, keepdims=True))
            inv_l = 1.0 / jnp.sum(p, axis=-1, keepdims=True)
            o_all = jnp.dot(p.astype(jnp.bfloat16), vx,
                            preferred_element_type=jnp.float32)
            tiles = []
            for t in range(GROUP_W // LANES):
                ra = slice(2 * t * GRID_W, (2 * t + 1) * GRID_W)
                rb = slice((2 * t + 1) * GRID_W, (2 * t + 2) * GRID_W)
                lt = slice(t * LANES, (t + 1) * LANES)
                tiles.append(jnp.where(low_half, o_all[ra, lt] * inv_l[ra],
                                       o_all[rb, lt] * inv_l[rb]))
            outs.append(jnp.concatenate(tiles, axis=1))
        ssq = sum(jnp.sum(o * o, axis=-1, keepdims=True) for o in outs)
        inv = lax.rsqrt(ssq / ATTN_DIM + RMS_EPS)
        for hg in range(N_GROUPS):
            lanes = slice(hg * GROUP_W, (hg + 1) * GROUP_W)
            o_ref[pl.ds(q0, GRID_W), lanes] = (
                outs[hg] * inv * g_ref[:, lanes]).astype(o_ref.dtype)
        return carry

    lax.fori_loop(0, ROWS_PER_STEP, row_body, 0, unroll=ROW_UNROLL)


def _attention(proj, proj_meta, bias, g, batch, seq):
    rows = seq // GRID_W
    steps = rows // ROWS_PER_STEP
    tq = ROWS_PER_STEP * GRID_W
    single = pl.Buffered(1)
    return pl.pallas_call(
        _attn_kernel,
        grid=(batch, steps),
        in_specs=[
            pl.BlockSpec((None, tq, PANEL), lambda b, s: (P_Q, b * steps + s, 0)),
            pl.BlockSpec((None, seq, PANEL), lambda b, s: (P_K, b, 0), pipeline_mode=single),
            pl.BlockSpec((None, seq, PANEL), lambda b, s: (P_V, b, 0), pipeline_mode=single),
            pl.BlockSpec((None, N_META, PANEL), lambda b, s: (P_K, 0, 0)),
            pl.BlockSpec((None, N_META, PANEL), lambda b, s: (P_V, 0, 0)),
            pl.BlockSpec(bias.shape, lambda b, s: (0, 0, 0, 0), pipeline_mode=single),
            pl.BlockSpec((1, ATTN_DIM), lambda b, s: (0, 0)),
        ],
        out_specs=pl.BlockSpec((tq, ATTN_DIM), lambda b, s: (b * steps + s, 0)),
        out_shape=jax.ShapeDtypeStruct((batch * seq, ATTN_DIM), jnp.bfloat16),
        scratch_shapes=[pltpu.VMEM((META_PAD, PANEL), jnp.bfloat16),
                        pltpu.VMEM((META_PAD, PANEL), jnp.bfloat16)],
        compiler_params=pltpu.CompilerParams(
            dimension_semantics=("arbitrary", "arbitrary"),
            vmem_limit_bytes=V7X_VMEM_LIMIT_BYTES),
        name="attention",
    )(proj, proj, proj, proj_meta, proj_meta, bias, g)


N_DR = 2 * WIN_ROWS - 1
N_DC = 2 * WIN_COLS - 1


N_PAIR = N_DR - 1


def _bias_kernel(rpb_ref, o_ref):
    h = pl.program_id(0)
    shape = (GRID_W, LANES)
    qc = lax.broadcasted_iota(jnp.int32, shape, 0)
    lane = lax.broadcasted_iota(jnp.int32, shape, 1)
    kc = lane % GRID_W
    dc = kc - qc + (WIN_COLS - 1)
    c0 = jnp.clip(qc - WIN_COLS // 2, 0, GRID_W - WIN_COLS)
    valid = (kc >= c0) & (kc < c0 + WIN_COLS)
    tiles = []
    for dr in range(N_DR):
        base = (h * N_DR + dr) * N_DC
        t = jnp.zeros(shape, jnp.float32)
        for j in range(N_DC):
            t = jnp.where(dc == j, rpb_ref[base + j], t)
        tiles.append(jnp.where(valid, t, NEG_INF))
    low_half = lane < GRID_W
    for dr0 in range(N_PAIR):
        o_ref[0, dr0] = jnp.where(low_half, tiles[dr0], tiles[dr0 + 1])


def _bias_table(rpb):
    return pl.pallas_call(
        _bias_kernel,
        grid=(ATTN_HEADS,),
        in_specs=[pl.BlockSpec(memory_space=pltpu.SMEM)],
        out_specs=pl.BlockSpec((1, N_PAIR, GRID_W, LANES), lambda h: (h, 0, 0, 0)),
        out_shape=jax.ShapeDtypeStruct((ATTN_HEADS, N_PAIR, GRID_W, LANES), jnp.float32),
        compiler_params=pltpu.CompilerParams(dimension_semantics=("arbitrary",)),
        name="bias_table",
    )(rpb.reshape(-1))


def _mix_kernel(b_ref, c_ref, u_ref, cp_ref, up_ref, cn_ref, un_ref, cm_ref, um_ref,
                ya_ref, x_ref, cw_ref, cb_ref, cg_ref, wo_ref, o_ref, *, blocks_per_seq):
    i = pl.program_id(0)
    tm = b_ref.shape[0]
    f32 = jnp.float32
    v = c_ref[...].astype(f32) * u_ref[...].astype(f32)
    last = BF16_SUBLANES - 1
    v_before = cp_ref[last:, :].astype(f32) * up_ref[last:, :].astype(f32)
    v_meta = cm_ref[N_META - 1:, :].astype(f32) * um_ref[N_META - 1:, :].astype(f32)
    v_after = cn_ref[0:1, :].astype(f32) * un_ref[0:1, :].astype(f32)
    seq_first = (i % blocks_per_seq) == 0
    seq_last = (i % blocks_per_seq) == blocks_per_seq - 1
    v_before = jnp.where(seq_first, v_meta, v_before)
    v_after = jnp.where(seq_last, 0.0, v_after)
    t = lax.broadcasted_iota(jnp.int32, (tm, 1), 0)
    v_prev = jnp.where(t == 0, v_before, pltpu.roll(v, 1, axis=0))
    v_next = jnp.where(t == tm - 1, v_after, pltpu.roll(v, tm - 1, axis=0))
    conv = (v_prev * cw_ref[0:1, :] + v * cw_ref[1:2, :] + v_next * cw_ref[2:3, :]
            + cb_ref[...])
    y = b_ref[...].astype(f32) * conv
    yc = _rms(y, cg_ref[...]).astype(jnp.bfloat16)
    acc = jnp.dot(yc, wo_ref[0:CONV_DIM, :], preferred_element_type=f32)
    acc = acc + jnp.dot(ya_ref[...], wo_ref[CONV_DIM:, :], preferred_element_type=f32)
    o_ref[...] = x_ref[...] + acc


def _mix(proj, proj_meta, y_attn, x2d, conv_w, conv_b, conv_g, w_out_bf16, seq, tm):
    rows = x2d.shape[0]
    nblk = rows // tm
    hb = tm // BF16_SUBLANES
    n_hb = rows // BF16_SUBLANES
    panel = lambda p: pl.BlockSpec((None, tm, PANEL), lambda i, p=p: (p, i, 0))
    prev = lambda p: pl.BlockSpec((None, BF16_SUBLANES, PANEL),
                                  lambda i, p=p: (p, jnp.maximum(i * hb - 1, 0), 0))
    nxt = lambda p: pl.BlockSpec((None, BF16_SUBLANES, PANEL),
                                 lambda i, p=p: (p, jnp.minimum((i + 1) * hb, n_hb - 1), 0))
    meta = lambda p: pl.BlockSpec((None, N_META, PANEL), lambda i, p=p: (p, 0, 0))
    vec = lambda n: pl.BlockSpec((n, CONV_DIM), lambda i: (0, 0))
    return pl.pallas_call(
        functools.partial(_mix_kernel, blocks_per_seq=seq // tm),
        grid=(nblk,),
        in_specs=[
            panel(P_B), panel(P_C), panel(P_U),
            prev(P_C), prev(P_U), nxt(P_C), nxt(P_U), meta(P_C), meta(P_U),
            pl.BlockSpec((tm, ATTN_DIM), lambda i: (i, 0)),
            pl.BlockSpec((tm, D_MODEL), lambda i: (i, 0)),
            vec(3), vec(1), vec(1),
            pl.BlockSpec((D_MODEL, D_MODEL), lambda i: (0, 0),
                         pipeline_mode=pl.Buffered(1)),
        ],
        out_specs=pl.BlockSpec((tm, D_MODEL), lambda i: (i, 0)),
        out_shape=jax.ShapeDtypeStruct((rows, D_MODEL), jnp.float32),
        compiler_params=pltpu.CompilerParams(
            dimension_semantics=("arbitrary",),
            vmem_limit_bytes=V7X_VMEM_LIMIT_BYTES),
        name="mix",
    )(proj, proj, proj, proj, proj, proj, proj, proj_meta, proj_meta,
      y_attn, x2d, conv_w, conv_b, conv_g, w_out_bf16)


def _mlp_step(f, nf, h_ref, g2_ref, wu_ref, wd_ref, gf_ref, o_ref, hn_ref):
    @pl.when(f == 0)
    def _():
        h = h_ref[...]
        hn_ref[...] = _rms(h, g2_ref[...]).astype(jnp.bfloat16)
        o_ref[...] = h

    wu = jnp.concatenate([wu_ref[s] for s in range(wu_ref.shape[0])], axis=1)
    a = jnp.dot(hn_ref[...], wu, preferred_element_type=jnp.float32)
    a = jnp.square(jnp.maximum(a, 0.0)).astype(jnp.bfloat16)
    o_ref[...] += jnp.dot(a, wd_ref[...], preferred_element_type=jnp.float32)

    @pl.when(f == nf - 1)
    def _():
        o_ref[...] = _rms(o_ref[...], gf_ref[...])


def _mlp_first_kernel(h_ref, g2_ref, wu_ref, wd_ref, gf_ref, o_ref, wub_ref, wdb_ref, hn_ref):
    slab_w = wub_ref.shape[2]
    for s in range(wub_ref.shape[0]):
        wub_ref[s] = wu_ref[:, s * slab_w:(s + 1) * slab_w].astype(jnp.bfloat16)
    wdb_ref[...] = wd_ref[...].astype(jnp.bfloat16)
    _mlp_step(pl.program_id(0), pl.num_programs(0),
              h_ref, g2_ref, wub_ref, wdb_ref, gf_ref, o_ref, hn_ref)


def _mlp_rest_kernel(h_ref, g2_ref, wu_ref, wd_ref, gf_ref, o_ref, hn_ref):
    _mlp_step(pl.program_id(1), pl.num_programs(1),
              h_ref, g2_ref, wu_ref, wd_ref, gf_ref, o_ref, hn_ref)


UP_SLAB = 256


def _mlp(h1, g2, w_up_f32, w_down_f32, gf, tm, tf):
    rows = h1.shape[0]
    n_slab = tf // UP_SLAB
    params = lambda sem: pltpu.CompilerParams(
        dimension_semantics=sem, vmem_limit_bytes=V7X_VMEM_LIMIT_BYTES)
    once = pl.Buffered(1)
    out, w_up_b, w_down_b = pl.pallas_call(
        _mlp_first_kernel,
        grid=(D_FF // tf,),
        in_specs=[
            pl.BlockSpec((tm, D_MODEL), lambda f: (0, 0), pipeline_mode=once),
            pl.BlockSpec((1, D_MODEL), lambda f: (0, 0)),
            pl.BlockSpec((D_MODEL, tf), lambda f: (0, f)),
            pl.BlockSpec((tf, D_MODEL), lambda f: (f, 0)),
            pl.BlockSpec((1, D_MODEL), lambda f: (0, 0)),
        ],
        out_specs=[
            pl.BlockSpec((tm, D_MODEL), lambda f: (0, 0), pipeline_mode=once),
            pl.BlockSpec((n_slab, D_MODEL, UP_SLAB), lambda f: (f, 0, 0)),
            pl.BlockSpec((tf, D_MODEL), lambda f: (f, 0)),
        ],
        out_shape=[
            jax.ShapeDtypeStruct((rows, D_MODEL), jnp.float32),
            jax.ShapeDtypeStruct((D_FF // UP_SLAB, D_MODEL, UP_SLAB), jnp.bfloat16),
            jax.ShapeDtypeStruct((D_FF, D_MODEL), jnp.bfloat16),
        ],
        scratch_shapes=[pltpu.VMEM((tm, D_MODEL), jnp.bfloat16)],
        input_output_aliases={0: 0},
        compiler_params=params(("arbitrary",)),
        name="mlp_first",
    )(h1, g2, w_up_f32, w_down_f32, gf)
    return pl.pallas_call(
        _mlp_rest_kernel,
        grid=(rows // tm - 1, D_FF // tf),
        in_specs=[
            pl.BlockSpec((tm, D_MODEL), lambda i, f: (i + 1, 0)),
            pl.BlockSpec((1, D_MODEL), lambda i, f: (0, 0)),
            pl.BlockSpec((n_slab, D_MODEL, UP_SLAB), lambda i, f: (f, 0, 0)),
            pl.BlockSpec((tf, D_MODEL), lambda i, f: (f, 0)),
            pl.BlockSpec((1, D_MODEL), lambda i, f: (0, 0)),
        ],
        out_specs=pl.BlockSpec((tm, D_MODEL), lambda i, f: (i + 1, 0)),
        out_shape=jax.ShapeDtypeStruct((rows, D_MODEL), jnp.float32),
        scratch_shapes=[pltpu.VMEM((tm, D_MODEL), jnp.bfloat16)],
        input_output_aliases={0: 0},
        compiler_params=params(("arbitrary", "arbitrary")),
        name="mlp_rest",
    )(out, g2, w_up_b, w_down_b, gf)


def kernel(x, meta_tokens, norm1_g, w_in, conv_w, conv_b, conv_norm_g, attn_rpb,
           attn_norm_g, w_out, norm2_g, w_up, w_down, final_norm_g):
    batch, seq, _ = x.shape
    x2d = x.reshape(batch * seq, D_MODEL)
    proj, proj_meta, w_out_b = _in_proj(x2d, meta_tokens, norm1_g, w_in[0], w_out[0], tm=1024)
    bias = _bias_table(attn_rpb[0])
    y_attn = _attention(proj, proj_meta, bias, attn_norm_g, batch, seq)
    h1 = _mix(proj, proj_meta, y_attn, x2d, conv_w[0], conv_b, conv_norm_g, w_out_b,
              seq, tm=512)
    out = _mlp(h1, norm2_g, w_up[0], w_down[0], final_norm_g.reshape(1, D_MODEL),
               tm=1024, tf=512)
    return out.reshape(batch, seq, D_MODEL)
```

```python
import functools

import jax
import jax.numpy as jnp
from jax import lax
from jax.experimental import pallas as pl
from jax.experimental.pallas import tpu as pltpu

D_MODEL = 2048
N_META = 16
CONV_DIM = D_MODEL // 2
ATTN_HEADS = 16
HEAD_DIM = 64
ATTN_DIM = ATTN_HEADS * HEAD_DIM
PROJ_DIM = 3 * CONV_DIM + 3 * ATTN_DIM
D_FF = 4 * D_MODEL
GRID_W = 64
WIN_ROWS = 8
WIN_COLS = 16
RMS_EPS = 1e-6
NEG_INF = -1e30

PANEL = 1024
P_B, P_C, P_U, P_Q, P_K, P_V = range(6)

V7X_VMEM_LIMIT_BYTES = 60 * 1024 * 1024
BF16_SUBLANES = 16
LANES = 128

HEADS_PER_GROUP = 4
GROUP_W = HEADS_PER_GROUP * HEAD_DIM
N_GROUPS = ATTN_HEADS // HEADS_PER_GROUP
ROWS_PER_STEP = 8
WIN_KEYS = WIN_ROWS * GRID_W
META_PAD = LANES
ALL_KEYS = WIN_KEYS + META_PAD
ROW_UNROLL = 8


def _rms(xf, g):
    return xf * lax.rsqrt(jnp.mean(xf * xf, axis=-1, keepdims=True) + RMS_EPS) * g


def _in_proj_first_kernel(x_ref, meta_ref, g_ref, w_ref, o_ref, om_ref, wb_ref, hn_ref):
    tm = x_ref.shape[0]

    @pl.when(pl.program_id(0) == 0)
    def _():
        hn_ref[0:tm, :] = _rms(x_ref[...], g_ref[...]).astype(jnp.bfloat16)
        hn_ref[tm:, :] = _rms(meta_ref[...], g_ref[...]).astype(jnp.bfloat16)

    w = w_ref[...].astype(jnp.bfloat16)
    wb_ref[...] = w
    r = jnp.dot(hn_ref[...], w, preferred_element_type=jnp.float32)
    o_ref[...] = r[0:tm].astype(o_ref.dtype)
    om_ref[...] = r[tm:].astype(om_ref.dtype)


def _in_proj_rest_kernel(x_ref, g_ref, w_ref, first_ref, wo_ref, o_ref, wob_ref, hn_ref):
    i = pl.program_id(0)

    @pl.when(i == 0)
    def _():
        o_ref[...] = first_ref[...]

    @pl.when(i > 0)
    def _():
        @pl.when(pl.program_id(1) == 0)
        def _():
            hn_ref[...] = _rms(x_ref[...], g_ref[...]).astype(jnp.bfloat16)

        o_ref[...] = jnp.dot(hn_ref[...], w_ref[...],
                             preferred_element_type=jnp.float32).astype(o_ref.dtype)

    wob_ref[...] = wo_ref[...].astype(jnp.bfloat16)


CAST_STEPS = 32


def _in_proj(x2d, meta, g, w_f32, w_out, tm):
    rows = x2d.shape[0]
    nj = PROJ_DIM // PANEL
    ni = rows // tm
    assert ni * nj >= CAST_STEPS
    slab = lambda i, j: jnp.minimum(i * nj + j, CAST_STEPS - 1)
    out_spec = pl.BlockSpec((D_MODEL // CAST_STEPS, D_MODEL), lambda i, j: (slab(i, j), 0))
    params = lambda sem: pltpu.CompilerParams(
        dimension_semantics=sem, vmem_limit_bytes=V7X_VMEM_LIMIT_BYTES)
    proj0, proj_meta, w_bf16 = pl.pallas_call(
        _in_proj_first_kernel,
        grid=(nj,),
        in_specs=[
            pl.BlockSpec((tm, D_MODEL), lambda j: (0, 0)),
            pl.BlockSpec((N_META, D_MODEL), lambda j: (0, 0)),
            pl.BlockSpec((1, D_MODEL), lambda j: (0, 0)),
            pl.BlockSpec((D_MODEL, PANEL), lambda j: (0, j)),
        ],
        out_specs=[
            pl.BlockSpec((None, tm, PANEL), lambda j: (j, 0, 0)),
            pl.BlockSpec((None, N_META, PANEL), lambda j: (j, 0, 0)),
            pl.BlockSpec((None, D_MODEL, PANEL), lambda j: (j, 0, 0)),
        ],
        out_shape=[
            jax.ShapeDtypeStruct((nj, tm, PANEL), jnp.bfloat16),
            jax.ShapeDtypeStruct((nj, N_META, PANEL), jnp.bfloat16),
            jax.ShapeDtypeStruct((nj, D_MODEL, PANEL), jnp.bfloat16),
        ],
        scratch_shapes=[pltpu.VMEM((tm + N_META, D_MODEL), jnp.bfloat16)],
        compiler_params=params(("arbitrary",)),
        name="in_proj_first",
    )(x2d, meta, g, w_f32)
    proj, w_out_b = pl.pallas_call(
        _in_proj_rest_kernel,
        grid=(ni, nj),
        in_specs=[
            pl.BlockSpec((tm, D_MODEL), lambda i, j: (jnp.maximum(i, 1), 0)),
            pl.BlockSpec((1, D_MODEL), lambda i, j: (0, 0)),
            pl.BlockSpec((None, D_MODEL, PANEL), lambda i, j: (jnp.where(i == 0, 0, j), 0, 0)),
            pl.BlockSpec((None, tm, PANEL), lambda i, j: (jnp.where(i == 0, j, nj - 1), 0, 0)),
            out_spec,
        ],
        out_specs=[pl.BlockSpec((None, tm, PANEL), lambda i, j: (j, i, 0)), out_spec],
        out_shape=[jax.ShapeDtypeStruct((nj, rows, PANEL), jnp.bfloat16),
                   jax.ShapeDtypeStruct(w_out.shape, jnp.bfloat16)],
        scratch_shapes=[pltpu.VMEM((tm, D_MODEL), jnp.bfloat16)],
        compiler_params=params(("arbitrary", "arbitrary")),
        name="in_proj_rest",
    )(x2d, g, w_bf16, proj0, w_out)
    return proj, proj_meta, w_out_b


def _attn_kernel(q_ref, k_ref, v_ref, km_ref, vm_ref, bias_ref, g_ref, o_ref,
                 kmp_ref, vmp_ref):
    grp = pl.program_id(1)

    kmp_ref[...] = jnp.zeros_like(kmp_ref)
    vmp_ref[...] = jnp.zeros_like(vmp_ref)
    kmp_ref[0:N_META, :] = km_ref[...]
    vmp_ref[0:N_META, :] = vm_ref[...]

    row_head = lax.broadcasted_iota(jnp.int32, (GROUP_W, GROUP_W), 0) // HEAD_DIM
    col_head = lax.broadcasted_iota(jnp.int32, (GROUP_W, GROUP_W), 1) // HEAD_DIM
    diag = row_head == col_head
    lane = lax.broadcasted_iota(jnp.int32, (GRID_W, LANES), 1)
    low_half = lane < HEAD_DIM
    meta_mask = jnp.where(lane < N_META, 0.0, NEG_INF)
    scale = HEAD_DIM ** -0.5
    nt = (((1,), (1,)), ((), ()))

    def row_body(rl, carry):
        r = grp * ROWS_PER_STEP + rl
        rs = jnp.clip(r - WIN_ROWS // 2, 0, GRID_W - WIN_ROWS)
        case = r - rs
        q0 = pl.multiple_of(rl * GRID_W, GRID_W)
        k0 = pl.multiple_of(rs * GRID_W, GRID_W)
        outs = []
        for hg in range(N_GROUPS):
            lanes = slice(hg * GROUP_W, (hg + 1) * GROUP_W)
            qg = q_ref[pl.ds(q0, GRID_W), lanes] * scale
            qbd = jnp.where(diag, jnp.concatenate([qg] * HEADS_PER_GROUP, axis=0),
                            jnp.zeros((), jnp.bfloat16))
            kx = jnp.concatenate(
                [k_ref[pl.ds(k0, WIN_KEYS), lanes], kmp_ref[:, lanes]], axis=0)
            vx = jnp.concatenate(
                [v_ref[pl.ds(k0, WIN_KEYS), lanes], vmp_ref[:, lanes]], axis=0)
            s = lax.dot_general(qbd, kx, nt, preferred_element_type=jnp.float32)
            bias = jnp.concatenate([
                jnp.concatenate(
                    [bias_ref[hg * HEADS_PER_GROUP + h, 2 * p + (WIN_ROWS - 1) - case]
                     for p in range(WIN_ROWS // 2)] + [meta_mask], axis=1)
                for h in range(HEADS_PER_GROUP)], axis=0)
            s = s + bias
            p = jnp.exp(s - jnp.max(s, axis=-1, keepdims=True))
            inv_l = 1.0 / jnp.sum(p, axis=-1, keepdims=True)
            o_all = jnp.dot(p.astype(jnp.bfloat16), vx,
                            preferred_element_type=jnp.float32)
            tiles = []
            for t in range(GROUP_W // LANES):
                ra = slice(2 * t * GRID_W, (2 * t + 1) * GRID_W)
                rb = slice((2 * t + 1) * GRID_W, (2 * t + 2) * GRID_W)
                lt = slice(t * LANES, (t + 1) * LANES)
                tiles.append(jnp.where(low_half, o_all[ra, lt] * inv_l[ra],
                                       o_all[rb, lt] * inv_l[rb]))
            outs.append(jnp.concatenate(tiles, axis=1))
        ssq = sum(jnp.sum(o * o, axis=-1, keepdims=True) for o in outs)
        inv = lax.rsqrt(ssq / ATTN_DIM + RMS_EPS)
        for hg in range(N_GROUPS):
            lanes = slice(hg * GROUP_W, (hg + 1) * GROUP_W)
            o_ref[pl.ds(q0, GRID_W), lanes] = (
                outs[hg] * inv * g_ref[:, lanes]).astype(o_ref.dtype)
        return carry

    lax.fori_loop(0, ROWS_PER_STEP, row_body, 0, unroll=ROW_UNROLL)


def _attention(proj, proj_meta, bias, g, batch, seq):
    rows = seq // GRID_W
    steps = rows // ROWS_PER_STEP
    tq = ROWS_PER_STEP * GRID_W
    single = pl.Buffered(1)
    return pl.pallas_call(
        _attn_kernel,
        grid=(batch, steps),
        in_specs=[
            pl.BlockSpec((None, tq, PANEL), lambda b, s: (P_Q, b * steps + s, 0)),
            pl.BlockSpec((None, seq, PANEL), lambda b, s: (P_K, b, 0), pipeline_mode=single),
            pl.BlockSpec((None, seq, PANEL), lambda b, s: (P_V, b, 0), pipeline_mode=single),
            pl.BlockSpec((None, N_META, PANEL), lambda b, s: (P_K, 0, 0)),
            pl.BlockSpec((None, N_META, PANEL), lambda b, s: (P_V, 0, 0)),
            pl.BlockSpec(bias.shape, lambda b, s: (0, 0, 0, 0), pipeline_mode=single),
            pl.BlockSpec((1, ATTN_DIM), lambda b, s: (0, 0)),
        ],
        out_specs=pl.BlockSpec((tq, ATTN_DIM), lambda b, s: (b * steps + s, 0)),
        out_shape=jax.ShapeDtypeStruct((batch * seq, ATTN_DIM), jnp.bfloat16),
        scratch_shapes=[pltpu.VMEM((META_PAD, PANEL), jnp.bfloat16),
                        pltpu.VMEM((META_PAD, PANEL), jnp.bfloat16)],
        compiler_params=pltpu.CompilerParams(
            dimension_semantics=("arbitrary", "arbitrary"),
            vmem_limit_bytes=V7X_VMEM_LIMIT_BYTES),
        name="attention",
    )(proj, proj, proj, proj_meta, proj_meta, bias, g)


N_DR = 2 * WIN_ROWS - 1
N_DC = 2 * WIN_COLS - 1


N_PAIR = N_DR - 1


def _bias_kernel(rpb_ref, o_ref):
    shape = (GRID_W, LANES)
    qc = lax.broadcasted_iota(jnp.int32, shape, 0)
    lane = lax.broadcasted_iota(jnp.int32, shape, 1)
    kc = lane % GRID_W
    c0 = jnp.clip(qc - WIN_COLS // 2, 0, GRID_W - WIN_COLS)
    valid = (kc >= c0) & (kc < c0 + WIN_COLS)
    low_half = lane < GRID_W

    def toeplitz(dr, lane0):
        row = jnp.broadcast_to(rpb_ref[0, dr:dr + 1, :], shape)
        return pltpu.roll(row, (lane0 - (WIN_COLS - 1)) % LANES, 1, stride=1, stride_axis=0)

    low = [toeplitz(dr, 0) for dr in range(N_PAIR)]
    high = [toeplitz(dr + 1, GRID_W) for dr in range(N_PAIR)]
    for dr0 in range(N_PAIR):
        o_ref[0, dr0] = jnp.where(valid, jnp.where(low_half, low[dr0], high[dr0]), NEG_INF)


def _bias_table(rpb):
    n_dr_pad = N_DR + 1
    rpb_rows = jnp.pad(rpb, ((0, 0), (0, n_dr_pad - N_DR), (0, LANES - N_DC)))
    return pl.pallas_call(
        _bias_kernel,
        grid=(ATTN_HEADS,),
        in_specs=[pl.BlockSpec((1, n_dr_pad, LANES), lambda h: (h, 0, 0))],
        out_specs=pl.BlockSpec((1, N_PAIR, GRID_W, LANES), lambda h: (h, 0, 0, 0)),
        out_shape=jax.ShapeDtypeStruct((ATTN_HEADS, N_PAIR, GRID_W, LANES), jnp.float32),
        compiler_params=pltpu.CompilerParams(dimension_semantics=("arbitrary",)),
        name="bias_table",
    )(rpb_rows)


def _mix_kernel(b_ref, c_ref, u_ref, cp_ref, up_ref, cn_ref, un_ref, cm_ref, um_ref,
                ya_ref, x_ref, cw_ref, cb_ref, cg_ref, wo_ref, o_ref, *, blocks_per_seq):
    i = pl.program_id(0)
    tm = b_ref.shape[0]
    f32 = jnp.float32
    v = c_ref[...].astype(f32) * u_ref[...].astype(f32)
    last = BF16_SUBLANES - 1
    v_before = cp_ref[last:, :].astype(f32) * up_ref[last:, :].astype(f32)
    v_meta = cm_ref[N_META - 1:, :].astype(f32) * um_ref[N_META - 1:, :].astype(f32)
    v_after = cn_ref[0:1, :].astype(f32) * un_ref[0:1, :].astype(f32)
    seq_first = (i % blocks_per_seq) == 0
    seq_last = (i % blocks_per_seq) == blocks_per_seq - 1
    v_before = jnp.where(seq_first, v_meta, v_before)
    v_after = jnp.where(seq_last, 0.0, v_after)
    t = lax.broadcasted_iota(jnp.int32, (tm, 1), 0)
    v_prev = jnp.where(t == 0, v_before, pltpu.roll(v, 1, axis=0))
    v_next = jnp.where(t == tm - 1, v_after, pltpu.roll(v, tm - 1, axis=0))
    conv = (v_prev * cw_ref[0:1, :] + v * cw_ref[1:2, :] + v_next * cw_ref[2:3, :]
            + cb_ref[...])
    y = b_ref[...].astype(f32) * conv
    yc = _rms(y, cg_ref[...]).astype(jnp.bfloat16)
    acc = jnp.dot(yc, wo_ref[0:CONV_DIM, :], preferred_element_type=f32)
    acc = acc + jnp.dot(ya_ref[...], wo_ref[CONV_DIM:, :], preferred_element_type=f32)
    o_ref[...] = x_ref[...] + acc


def _mix(proj, proj_meta, y_attn, x2d, conv_w, conv_b, conv_g, w_out_bf16, seq, tm):
    rows = x2d.shape[0]
    nblk = rows // tm
    hb = tm // BF16_SUBLANES
    n_hb = rows // BF16_SUBLANES
    panel = lambda p: pl.BlockSpec((None, tm, PANEL), lambda i, p=p: (p, i, 0))
    prev = lambda p: pl.BlockSpec((None, BF16_SUBLANES, PANEL),
                                  lambda i, p=p: (p, jnp.maximum(i * hb - 1, 0), 0))
    nxt = lambda p: pl.BlockSpec((None, BF16_SUBLANES, PANEL),
                                 lambda i, p=p: (p, jnp.minimum((i + 1) * hb, n_hb - 1), 0))
    meta = lambda p: pl.BlockSpec((None, N_META, PANEL), lambda i, p=p: (p, 0, 0))
    vec = lambda n: pl.BlockSpec((n, CONV_DIM), lambda i: (0, 0))
    return pl.pallas_call(
        functools.partial(_mix_kernel, blocks_per_seq=seq // tm),
        grid=(nblk,),
        in_specs=[
            panel(P_B), panel(P_C), panel(P_U),
            prev(P_C), prev(P_U), nxt(P_C), nxt(P_U), meta(P_C), meta(P_U),
            pl.BlockSpec((tm, ATTN_DIM), lambda i: (i, 0)),
            pl.BlockSpec((tm, D_MODEL), lambda i: (i, 0)),
            vec(3), vec(1), vec(1),
            pl.BlockSpec((D_MODEL, D_MODEL), lambda i: (0, 0),
                         pipeline_mode=pl.Buffered(1)),
        ],
        out_specs=pl.BlockSpec((tm, D_MODEL), lambda i: (i, 0)),
        out_shape=jax.ShapeDtypeStruct((rows, D_MODEL), jnp.float32),
        compiler_params=pltpu.CompilerParams(
            dimension_semantics=("arbitrary",),
            vmem_limit_bytes=V7X_VMEM_LIMIT_BYTES),
        name="mix",
    )(proj, proj, proj, proj, proj, proj, proj, proj_meta, proj_meta,
      y_attn, x2d, conv_w, conv_b, conv_g, w_out_bf16)


def _mlp_step(f, nf, h_ref, g2_ref, wu_ref, wd_ref, gf_ref, o_ref, hn_ref):
    @pl.when(f == 0)
    def _():
        h = h_ref[...]
        hn_ref[...] = _rms(h, g2_ref[...]).astype(jnp.bfloat16)
        o_ref[...] = h

    wu = jnp.concatenate([wu_ref[s] for s in range(wu_ref.shape[0])], axis=1)
    a = jnp.dot(hn_ref[...], wu, preferred_element_type=jnp.float32)
    a = jnp.square(jnp.maximum(a, 0.0)).astype(jnp.bfloat16)
    o_ref[...] += jnp.dot(a, wd_ref[...], preferred_element_type=jnp.float32)

    @pl.when(f == nf - 1)
    def _():
        o_ref[...] = _rms(o_ref[...], gf_ref[...])


def _mlp_first_kernel(h_ref, g2_ref, wu_ref, wd_ref, gf_ref, o_ref, wub_ref, wdb_ref, hn_ref):
    slab_w = wub_ref.shape[2]
    for s in range(wub_ref.shape[0]):
        wub_ref[s] = wu_ref[:, s * slab_w:(s + 1) * slab_w].astype(jnp.bfloat16)
    wdb_ref[...] = wd_ref[...].astype(jnp.bfloat16)
    _mlp_step(pl.program_id(0), pl.num_programs(0),
              h_ref, g2_ref, wub_ref, wdb_ref, gf_ref, o_ref, hn_ref)


def _mlp_rest_kernel(h_ref, g2_ref, wu_ref, wd_ref, gf_ref, o_ref, hn_ref):
    _mlp_step(pl.program_id(1), pl.num_programs(1),
              h_ref, g2_ref, wu_ref, wd_ref, gf_ref, o_ref, hn_ref)


UP_SLAB = 256


def _mlp(h1, g2, w_up_f32, w_down_f32, gf, tm, tf, tf_rest):
    rows = h1.shape[0]
    n_slab = tf // UP_SLAB
    params = lambda sem: pltpu.CompilerParams(
        dimension_semantics=sem, vmem_limit_bytes=V7X_VMEM_LIMIT_BYTES)
    once = pl.Buffered(1)
    out, w_up_b, w_down_b = pl.pallas_call(
        _mlp_first_kernel,
        grid=(D_FF // tf,),
        in_specs=[
            pl.BlockSpec((tm, D_MODEL), lambda f: (0, 0), pipeline_mode=once),
            pl.BlockSpec((1, D_MODEL), lambda f: (0, 0)),
            pl.BlockSpec((D_MODEL, tf), lambda f: (0, f)),
            pl.BlockSpec((tf, D_MODEL), lambda f: (f, 0)),
            pl.BlockSpec((1, D_MODEL), lambda f: (0, 0)),
        ],
        out_specs=[
            pl.BlockSpec((tm, D_MODEL), lambda f: (0, 0), pipeline_mode=once),
            pl.BlockSpec((n_slab, D_MODEL, UP_SLAB), lambda f: (f, 0, 0)),
            pl.BlockSpec((tf, D_MODEL), lambda f: (f, 0)),
        ],
        out_shape=[
            jax.ShapeDtypeStruct((rows, D_MODEL), jnp.float32),
            jax.ShapeDtypeStruct((D_FF // UP_SLAB, D_MODEL, UP_SLAB), jnp.bfloat16),
            jax.ShapeDtypeStruct((D_FF, D_MODEL), jnp.bfloat16),
        ],
        scratch_shapes=[pltpu.VMEM((tm, D_MODEL), jnp.bfloat16)],
        input_output_aliases={0: 0},
        compiler_params=params(("arbitrary",)),
        name="mlp_first",
    )(h1, g2, w_up_f32, w_down_f32, gf)
    return pl.pallas_call(
        _mlp_rest_kernel,
        grid=(rows // tm - 1, D_FF // tf_rest),
        in_specs=[
            pl.BlockSpec((tm, D_MODEL), lambda i, f: (i + 1, 0)),
            pl.BlockSpec((1, D_MODEL), lambda i, f: (0, 0)),
            pl.BlockSpec((tf_rest // UP_SLAB, D_MODEL, UP_SLAB), lambda i, f: (f, 0, 0)),
            pl.BlockSpec((tf_rest, D_MODEL), lambda i, f: (f, 0)),
            pl.BlockSpec((1, D_MODEL), lambda i, f: (0, 0)),
        ],
        out_specs=pl.BlockSpec((tm, D_MODEL), lambda i, f: (i + 1, 0)),
        out_shape=jax.ShapeDtypeStruct((rows, D_MODEL), jnp.float32),
        scratch_shapes=[pltpu.VMEM((tm, D_MODEL), jnp.bfloat16)],
        input_output_aliases={0: 0},
        compiler_params=params(("arbitrary", "arbitrary")),
        name="mlp_rest",
    )(out, g2, w_up_b, w_down_b, gf)


def kernel(x, meta_tokens, norm1_g, w_in, conv_w, conv_b, conv_norm_g, attn_rpb,
           attn_norm_g, w_out, norm2_g, w_up, w_down, final_norm_g):
    batch, seq, _ = x.shape
    x2d = x.reshape(batch * seq, D_MODEL)
    proj, proj_meta, w_out_b = _in_proj(x2d, meta_tokens, norm1_g, w_in[0], w_out[0], tm=1024)
    bias = _bias_table(attn_rpb[0])
    y_attn = _attention(proj, proj_meta, bias, attn_norm_g, batch, seq)
    h1 = _mix(proj, proj_meta, y_attn, x2d, conv_w[0], conv_b, conv_norm_g, w_out_b,
              seq, tm=512)
    out = _mlp(h1, norm2_g, w_up[0], w_down[0], final_norm_g.reshape(1, D_MODEL),
               tm=1024, tf=512, tf_rest=1024)
    return out.reshape(batch, seq, D_MODEL)
```

```python
import functools

import jax
import jax.numpy as jnp
from jax import lax
from jax.experimental import pallas as pl
from jax.experimental.pallas import tpu as pltpu

D_MODEL = 2048
N_META = 16
CONV_DIM = D_MODEL // 2
ATTN_HEADS = 16
HEAD_DIM = 64
ATTN_DIM = ATTN_HEADS * HEAD_DIM
PROJ_DIM = 3 * CONV_DIM + 3 * ATTN_DIM
D_FF = 4 * D_MODEL
GRID_W = 64
WIN_ROWS = 8
WIN_COLS = 16
RMS_EPS = 1e-6
NEG_INF = -1e30

PANEL = 1024
P_B, P_C, P_U, P_Q, P_K, P_V = range(6)

V7X_VMEM_LIMIT_BYTES = 60 * 1024 * 1024
BF16_SUBLANES = 16
LANES = 128

HEADS_PER_GROUP = 4
GROUP_W = HEADS_PER_GROUP * HEAD_DIM
N_GROUPS = ATTN_HEADS // HEADS_PER_GROUP
ROWS_PER_STEP = 8
WIN_KEYS = WIN_ROWS * GRID_W
META_PAD = LANES
ALL_KEYS = WIN_KEYS + META_PAD
ROW_UNROLL = 8


def _rms(xf, g):
    return xf * lax.rsqrt(jnp.mean(xf * xf, axis=-1, keepdims=True) + RMS_EPS) * g


def _in_proj_first_kernel(x_ref, meta_ref, g_ref, w_ref, o_ref, om_ref, wb_ref, hn_ref):
    tm = x_ref.shape[0]

    @pl.when(pl.program_id(0) == 0)
    def _():
        hn_ref[0:tm, :] = _rms(x_ref[...], g_ref[...]).astype(jnp.bfloat16)
        hn_ref[tm:, :] = _rms(meta_ref[...], g_ref[...]).astype(jnp.bfloat16)

    w = w_ref[...].astype(jnp.bfloat16)
    wb_ref[...] = w
    r = jnp.dot(hn_ref[...], w, preferred_element_type=jnp.float32)
    o_ref[...] = r[0:tm].astype(o_ref.dtype)
    om_ref[...] = r[tm:].astype(om_ref.dtype)


def _in_proj_rest_kernel(x_ref, g_ref, w_ref, first_ref, wo_ref, o_ref, wob_ref, hn_ref):
    i = pl.program_id(0)

    @pl.when(i == 0)
    def _():
        o_ref[...] = first_ref[...]

    @pl.when(i > 0)
    def _():
        @pl.when(pl.program_id(1) == 0)
        def _():
            hn_ref[...] = _rms(x_ref[...], g_ref[...]).astype(jnp.bfloat16)

        o_ref[...] = jnp.dot(hn_ref[...], w_ref[...],
                             preferred_element_type=jnp.float32).astype(o_ref.dtype)

    wob_ref[...] = wo_ref[...].astype(jnp.bfloat16)


CAST_STEPS = 32


def _in_proj(x2d, meta, g, w_f32, w_out, tm):
    rows = x2d.shape[0]
    nj = PROJ_DIM // PANEL
    ni = rows // tm
    assert ni * nj >= CAST_STEPS
    slab = lambda i, j: jnp.minimum(i * nj + j, CAST_STEPS - 1)
    out_spec = pl.BlockSpec((D_MODEL // CAST_STEPS, D_MODEL), lambda i, j: (slab(i, j), 0))
    params = lambda sem: pltpu.CompilerParams(
        dimension_semantics=sem, vmem_limit_bytes=V7X_VMEM_LIMIT_BYTES)
    proj0, proj_meta, w_bf16 = pl.pallas_call(
        _in_proj_first_kernel,
        grid=(nj,),
        in_specs=[
            pl.BlockSpec((tm, D_MODEL), lambda j: (0, 0)),
            pl.BlockSpec((N_META, D_MODEL), lambda j: (0, 0)),
            pl.BlockSpec((1, D_MODEL), lambda j: (0, 0)),
            pl.BlockSpec((D_MODEL, PANEL), lambda j: (0, j)),
        ],
        out_specs=[
            pl.BlockSpec((None, tm, PANEL), lambda j: (j, 0, 0)),
            pl.BlockSpec((None, N_META, PANEL), lambda j: (j, 0, 0)),
            pl.BlockSpec((None, D_MODEL, PANEL), lambda j: (j, 0, 0)),
        ],
        out_shape=[
            jax.ShapeDtypeStruct((nj, tm, PANEL), jnp.bfloat16),
            jax.ShapeDtypeStruct((nj, N_META, PANEL), jnp.bfloat16),
            jax.ShapeDtypeStruct((nj, D_MODEL, PANEL), jnp.bfloat16),
        ],
        scratch_shapes=[pltpu.VMEM((tm + N_META, D_MODEL), jnp.bfloat16)],
        compiler_params=params(("arbitrary",)),
        name="in_proj_first",
    )(x2d, meta, g, w_f32)
    proj, w_out_b = pl.pallas_call(
        _in_proj_rest_kernel,
        grid=(ni, nj),
        in_specs=[
            pl.BlockSpec((tm, D_MODEL), lambda i, j: (jnp.maximum(i, 1), 0)),
            pl.BlockSpec((1, D_MODEL), lambda i, j: (0, 0)),
            pl.BlockSpec((None, D_MODEL, PANEL), lambda i, j: (jnp.where(i == 0, 0, j), 0, 0)),
            pl.BlockSpec((None, tm, PANEL), lambda i, j: (jnp.where(i == 0, j, nj - 1), 0, 0)),
            out_spec,
        ],
        out_specs=[pl.BlockSpec((None, tm, PANEL), lambda i, j: (j, i, 0)), out_spec],
        out_shape=[jax.ShapeDtypeStruct((nj, rows, PANEL), jnp.bfloat16),
                   jax.ShapeDtypeStruct(w_out.shape, jnp.bfloat16)],
        scratch_shapes=[pltpu.VMEM((tm, D_MODEL), jnp.bfloat16)],
        compiler_params=params(("arbitrary", "arbitrary")),
        name="in_proj_rest",
    )(x2d, g, w_bf16, proj0, w_out)
    return proj, proj_meta, w_out_b


def _attn_kernel(q_ref, k_ref, v_ref, km_ref, vm_ref, bias_ref, g_ref, wd_ref, o_ref, wdb_ref,
                 kmp_ref, vmp_ref):
    grp = pl.program_id(1)
    wdb_ref[...] = wd_ref[...].astype(jnp.bfloat16)

    kmp_ref[...] = jnp.zeros_like(kmp_ref)
    vmp_ref[...] = jnp.zeros_like(vmp_ref)
    kmp_ref[0:N_META, :] = km_ref[...]
    vmp_ref[0:N_META, :] = vm_ref[...]

    row_head = lax.broadcasted_iota(jnp.int32, (GROUP_W, GROUP_W), 0) // HEAD_DIM
    col_head = lax.broadcasted_iota(jnp.int32, (GROUP_W, GROUP_W), 1) // HEAD_DIM
    diag = row_head == col_head
    lane = lax.broadcasted_iota(jnp.int32, (GRID_W, LANES), 1)
    low_half = lane < HEAD_DIM
    meta_mask = jnp.where(lane < N_META, 0.0, NEG_INF)
    scale = HEAD_DIM ** -0.5
    nt = (((1,), (1,)), ((), ()))

    def row_body(rl, carry):
        r = grp * ROWS_PER_STEP + rl
        rs = jnp.clip(r - WIN_ROWS // 2, 0, GRID_W - WIN_ROWS)
        case = r - rs
        q0 = pl.multiple_of(rl * GRID_W, GRID_W)
        k0 = pl.multiple_of(rs * GRID_W, GRID_W)
        outs = []
        for hg in range(N_GROUPS):
            lanes = slice(hg * GROUP_W, (hg + 1) * GROUP_W)
            qg = q_ref[pl.ds(q0, GRID_W), lanes] * scale
            qbd = jnp.where(diag, jnp.concatenate([qg] * HEADS_PER_GROUP, axis=0),
                            jnp.zeros((), jnp.bfloat16))
            kx = jnp.concatenate(
                [k_ref[pl.ds(k0, WIN_KEYS), lanes], kmp_ref[:, lanes]], axis=0)
            vx = jnp.concatenate(
                [v_ref[pl.ds(k0, WIN_KEYS), lanes], vmp_ref[:, lanes]], axis=0)
            s = lax.dot_general(qbd, kx, nt, preferred_element_type=jnp.float32)
            bias = jnp.concatenate([
                jnp.concatenate(
                    [bias_ref[hg * HEADS_PER_GROUP + h, 2 * p + (WIN_ROWS - 1) - case]
                     for p in range(WIN_ROWS // 2)] + [meta_mask], axis=1)
                for h in range(HEADS_PER_GROUP)], axis=0)
            s = s + bias
            p = jnp.exp(s - jnp.max(s, axis=-1, keepdims=True))
            inv_l = 1.0 / jnp.sum(p, axis=-1, keepdims=True)
            o_all = jnp.dot(p.astype(jnp.bfloat16), vx,
                            preferred_element_type=jnp.float32)
            tiles = []
            for t in range(GROUP_W // LANES):
                ra = slice(2 * t * GRID_W, (2 * t + 1) * GRID_W)
                rb = slice((2 * t + 1) * GRID_W, (2 * t + 2) * GRID_W)
                lt = slice(t * LANES, (t + 1) * LANES)
                tiles.append(jnp.where(low_half, o_all[ra, lt] * inv_l[ra],
                                       o_all[rb, lt] * inv_l[rb]))
            outs.append(jnp.concatenate(tiles, axis=1))
        ssq = sum(jnp.sum(o * o, axis=-1, keepdims=True) for o in outs)
        inv = lax.rsqrt(ssq / ATTN_DIM + RMS_EPS)
        for hg in range(N_GROUPS):
            lanes = slice(hg * GROUP_W, (hg + 1) * GROUP_W)
            o_ref[pl.ds(q0, GRID_W), lanes] = (
                outs[hg] * inv * g_ref[:, lanes]).astype(o_ref.dtype)
        return carry

    lax.fori_loop(0, ROWS_PER_STEP, row_body, 0, unroll=ROW_UNROLL)


def _attention(proj, proj_meta, bias, g, w_down, batch, seq):
    rows = seq // GRID_W
    steps = rows // ROWS_PER_STEP
    tq = ROWS_PER_STEP * GRID_W
    single = pl.Buffered(1)
    slab = pl.BlockSpec((D_FF // (batch * steps), D_MODEL), lambda b, s: (b * steps + s, 0))
    return pl.pallas_call(
        _attn_kernel,
        grid=(batch, steps),
        in_specs=[
            pl.BlockSpec((None, tq, PANEL), lambda b, s: (P_Q, b * steps + s, 0)),
            pl.BlockSpec((None, seq, PANEL), lambda b, s: (P_K, b, 0), pipeline_mode=single),
            pl.BlockSpec((None, seq, PANEL), lambda b, s: (P_V, b, 0), pipeline_mode=single),
            pl.BlockSpec((None, N_META, PANEL), lambda b, s: (P_K, 0, 0)),
            pl.BlockSpec((None, N_META, PANEL), lambda b, s: (P_V, 0, 0)),
            pl.BlockSpec(bias.shape, lambda b, s: (0, 0, 0, 0), pipeline_mode=single),
            pl.BlockSpec((1, ATTN_DIM), lambda b, s: (0, 0)),
            slab,
        ],
        out_specs=[pl.BlockSpec((tq, ATTN_DIM), lambda b, s: (b * steps + s, 0)), slab],
        out_shape=[jax.ShapeDtypeStruct((batch * seq, ATTN_DIM), jnp.bfloat16),
                   jax.ShapeDtypeStruct(w_down.shape, jnp.bfloat16)],
        scratch_shapes=[pltpu.VMEM((META_PAD, PANEL), jnp.bfloat16),
                        pltpu.VMEM((META_PAD, PANEL), jnp.bfloat16)],
        compiler_params=pltpu.CompilerParams(
            dimension_semantics=("arbitrary", "arbitrary"),
            vmem_limit_bytes=V7X_VMEM_LIMIT_BYTES),
        name="attention",
    )(proj, proj, proj, proj_meta, proj_meta, bias, g, w_down)


N_DR = 2 * WIN_ROWS - 1
N_DC = 2 * WIN_COLS - 1


N_PAIR = N_DR - 1


def _bias_kernel(rpb_ref, o_ref):
    shape = (GRID_W, LANES)
    qc = lax.broadcasted_iota(jnp.int32, shape, 0)
    lane = lax.broadcasted_iota(jnp.int32, shape, 1)
    kc = lane % GRID_W
    c0 = jnp.clip(qc - WIN_COLS // 2, 0, GRID_W - WIN_COLS)
    valid = (kc >= c0) & (kc < c0 + WIN_COLS)
    low_half = lane < GRID_W

    def toeplitz(dr, lane0):
        row = jnp.broadcast_to(rpb_ref[0, dr:dr + 1, :], shape)
        return pltpu.roll(row, (lane0 - (WIN_COLS - 1)) % LANES, 1, stride=1, stride_axis=0)

    low = [toeplitz(dr, 0) for dr in range(N_PAIR)]
    high = [toeplitz(dr + 1, GRID_W) for dr in range(N_PAIR)]
    for dr0 in range(N_PAIR):
        o_ref[0, dr0] = jnp.where(valid, jnp.where(low_half, low[dr0], high[dr0]), NEG_INF)


def _bias_table(rpb):
    n_dr_pad = N_DR + 1
    rpb_rows = jnp.pad(rpb, ((0, 0), (0, n_dr_pad - N_DR), (0, LANES - N_DC)))
    return pl.pallas_call(
        _bias_kernel,
        grid=(ATTN_HEADS,),
        in_specs=[pl.BlockSpec((1, n_dr_pad, LANES), lambda h: (h, 0, 0))],
        out_specs=pl.BlockSpec((1, N_PAIR, GRID_W, LANES), lambda h: (h, 0, 0, 0)),
        out_shape=jax.ShapeDtypeStruct((ATTN_HEADS, N_PAIR, GRID_W, LANES), jnp.float32),
        compiler_params=pltpu.CompilerParams(dimension_semantics=("arbitrary",)),
        name="bias_table",
    )(rpb_rows)


def _mix_kernel(b_ref, c_ref, u_ref, cp_ref, up_ref, cn_ref, un_ref, cm_ref, um_ref,
                ya_ref, x_ref, cw_ref, cb_ref, cg_ref, wo_ref, o_ref, *, blocks_per_seq):
    i = pl.program_id(0)
    tm = b_ref.shape[0]
    f32 = jnp.float32
    v = c_ref[...].astype(f32) * u_ref[...].astype(f32)
    last = BF16_SUBLANES - 1
    v_before = cp_ref[last:, :].astype(f32) * up_ref[last:, :].astype(f32)
    v_meta = cm_ref[N_META - 1:, :].astype(f32) * um_ref[N_META - 1:, :].astype(f32)
    v_after = cn_ref[0:1, :].astype(f32) * un_ref[0:1, :].astype(f32)
    seq_first = (i % blocks_per_seq) == 0
    seq_last = (i % blocks_per_seq) == blocks_per_seq - 1
    v_before = jnp.where(seq_first, v_meta, v_before)
    v_after = jnp.where(seq_last, 0.0, v_after)
    t = lax.broadcasted_iota(jnp.int32, (tm, 1), 0)
    v_prev = jnp.where(t == 0, v_before, pltpu.roll(v, 1, axis=0))
    v_next = jnp.where(t == tm - 1, v_after, pltpu.roll(v, tm - 1, axis=0))
    conv = (v_prev * cw_ref[0:1, :] + v * cw_ref[1:2, :] + v_next * cw_ref[2:3, :]
            + cb_ref[...])
    y = b_ref[...].astype(f32) * conv
    yc = _rms(y, cg_ref[...]).astype(jnp.bfloat16)
    acc = jnp.dot(yc, wo_ref[0:CONV_DIM, :], preferred_element_type=f32)
    acc = acc + jnp.dot(ya_ref[...], wo_ref[CONV_DIM:, :], preferred_element_type=f32)
    o_ref[...] = x_ref[...] + acc


def _mix(proj, proj_meta, y_attn, x2d, conv_w, conv_b, conv_g, w_out_bf16, seq, tm):
    rows = x2d.shape[0]
    nblk = rows // tm
    hb = tm // BF16_SUBLANES
    n_hb = rows // BF16_SUBLANES
    panel = lambda p: pl.BlockSpec((None, tm, PANEL), lambda i, p=p: (p, i, 0))
    prev = lambda p: pl.BlockSpec((None, BF16_SUBLANES, PANEL),
                                  lambda i, p=p: (p, jnp.maximum(i * hb - 1, 0), 0))
    nxt = lambda p: pl.BlockSpec((None, BF16_SUBLANES, PANEL),
                                 lambda i, p=p: (p, jnp.minimum((i + 1) * hb, n_hb - 1), 0))
    meta = lambda p: pl.BlockSpec((None, N_META, PANEL), lambda i, p=p: (p, 0, 0))
    vec = lambda n: pl.BlockSpec((n, CONV_DIM), lambda i: (0, 0))
    return pl.pallas_call(
        functools.partial(_mix_kernel, blocks_per_seq=seq // tm),
        grid=(nblk,),
        in_specs=[
            panel(P_B), panel(P_C), panel(P_U),
            prev(P_C), prev(P_U), nxt(P_C), nxt(P_U), meta(P_C), meta(P_U),
            pl.BlockSpec((tm, ATTN_DIM), lambda i: (i, 0)),
            pl.BlockSpec((tm, D_MODEL), lambda i: (i, 0)),
            vec(3), vec(1), vec(1),
            pl.BlockSpec((D_MODEL, D_MODEL), lambda i: (0, 0),
                         pipeline_mode=pl.Buffered(1)),
        ],
        out_specs=pl.BlockSpec((tm, D_MODEL), lambda i: (i, 0)),
        out_shape=jax.ShapeDtypeStruct((rows, D_MODEL), jnp.float32),
        compiler_params=pltpu.CompilerParams(
            dimension_semantics=("arbitrary",),
            vmem_limit_bytes=V7X_VMEM_LIMIT_BYTES),
        name="mix",
    )(proj, proj, proj, proj, proj, proj, proj, proj_meta, proj_meta,
      y_attn, x2d, conv_w, conv_b, conv_g, w_out_bf16)


def _mlp_step(f, nf, h_ref, g2_ref, wu_ref, wd_ref, gf_ref, o_ref, hn_ref):
    @pl.when(f == 0)
    def _():
        h = h_ref[...]
        hn_ref[...] = _rms(h, g2_ref[...]).astype(jnp.bfloat16)
        o_ref[...] = h

    wu = jnp.concatenate([wu_ref[s] for s in range(wu_ref.shape[0])], axis=1)
    a = jnp.dot(hn_ref[...], wu, preferred_element_type=jnp.float32)
    a = jnp.square(jnp.maximum(a, 0.0)).astype(jnp.bfloat16)
    o_ref[...] += jnp.dot(a, wd_ref[...], preferred_element_type=jnp.float32)

    @pl.when(f == nf - 1)
    def _():
        o_ref[...] = _rms(o_ref[...], gf_ref[...])


def _mlp_first_kernel(h_ref, g2_ref, wu_ref, wd_ref, gf_ref, o_ref, wub_ref, hn_ref):
    slab_w = wub_ref.shape[2]
    for s in range(wub_ref.shape[0]):
        wub_ref[s] = wu_ref[:, s * slab_w:(s + 1) * slab_w].astype(jnp.bfloat16)
    _mlp_step(pl.program_id(0), pl.num_programs(0),
              h_ref, g2_ref, wub_ref, wd_ref, gf_ref, o_ref, hn_ref)


def _mlp_rest_kernel(h_ref, g2_ref, wu_ref, wd_ref, gf_ref, o_ref, hn_ref):
    _mlp_step(pl.program_id(1), pl.num_programs(1),
              h_ref, g2_ref, wu_ref, wd_ref, gf_ref, o_ref, hn_ref)


UP_SLAB = 256


def _mlp(h1, g2, w_up_f32, w_down_b, gf, tm, tf, tf_rest):
    rows = h1.shape[0]
    n_slab = tf // UP_SLAB
    params = lambda sem: pltpu.CompilerParams(
        dimension_semantics=sem, vmem_limit_bytes=V7X_VMEM_LIMIT_BYTES)
    once = pl.Buffered(1)
    out, w_up_b = pl.pallas_call(
        _mlp_first_kernel,
        grid=(D_FF // tf,),
        in_specs=[
            pl.BlockSpec((tm, D_MODEL), lambda f: (0, 0), pipeline_mode=once),
            pl.BlockSpec((1, D_MODEL), lambda f: (0, 0)),
            pl.BlockSpec((D_MODEL, tf), lambda f: (0, f)),
            pl.BlockSpec((tf, D_MODEL), lambda f: (f, 0)),
            pl.BlockSpec((1, D_MODEL), lambda f: (0, 0)),
        ],
        out_specs=[
            pl.BlockSpec((tm, D_MODEL), lambda f: (0, 0), pipeline_mode=once),
            pl.BlockSpec((n_slab, D_MODEL, UP_SLAB), lambda f: (f, 0, 0)),
        ],
        out_shape=[
            jax.ShapeDtypeStruct((rows, D_MODEL), jnp.float32),
            jax.ShapeDtypeStruct((D_FF // UP_SLAB, D_MODEL, UP_SLAB), jnp.bfloat16),
        ],
        scratch_shapes=[pltpu.VMEM((tm, D_MODEL), jnp.bfloat16)],
        input_output_aliases={0: 0},
        compiler_params=params(("arbitrary",)),
        name="mlp_first",
    )(h1, g2, w_up_f32, w_down_b, gf)
    return pl.pallas_call(
        _mlp_rest_kernel,
        grid=(rows // tm - 1, D_FF // tf_rest),
        in_specs=[
            pl.BlockSpec((tm, D_MODEL), lambda i, f: (i + 1, 0)),
            pl.BlockSpec((1, D_MODEL), lambda i, f: (0, 0)),
            pl.BlockSpec((tf_rest // UP_SLAB, D_MODEL, UP_SLAB), lambda i, f: (f, 0, 0)),
            pl.BlockSpec((tf_rest, D_MODEL), lambda i, f: (f, 0)),
            pl.BlockSpec((1, D_MODEL), lambda i, f: (0, 0)),
        ],
        out_specs=pl.BlockSpec((tm, D_MODEL), lambda i, f: (i + 1, 0)),
        out_shape=jax.ShapeDtypeStruct((rows, D_MODEL), jnp.float32),
        scratch_shapes=[pltpu.VMEM((tm, D_MODEL), jnp.bfloat16)],
        input_output_aliases={0: 0},
        compiler_params=params(("arbitrary", "arbitrary")),
        name="mlp_rest",
    )(out, g2, w_up_b, w_down_b, gf)


def kernel(x, meta_tokens, norm1_g, w_in, conv_w, conv_b, conv_norm_g, attn_rpb,
           attn_norm_g, w_out, norm2_g, w_up, w_down, final_norm_g):
    batch, seq, _ = x.shape
    x2d = x.reshape(batch * seq, D_MODEL)
    proj, proj_meta, w_out_b = _in_proj(x2d, meta_tokens, norm1_g, w_in[0], w_out[0], tm=1024)
    bias = _bias_table(attn_rpb[0])
    y_attn, w_down_b = _attention(proj, proj_meta, bias, attn_norm_g, w_down[0], batch, seq)
    h1 = _mix(proj, proj_meta, y_attn, x2d, conv_w[0], conv_b, conv_norm_g, w_out_b,
              seq, tm=512)
    out = _mlp(h1, norm2_g, w_up[0], w_down_b, final_norm_g.reshape(1, D_MODEL),
               tm=1024, tf=512, tf_rest=1024)
    return out.reshape(batch, seq, D_MODEL)
```

```python
import functools

import jax
import jax.numpy as jnp
from jax import lax
from jax.experimental import pallas as pl
from jax.experimental.pallas import tpu as pltpu

D_MODEL = 2048
N_META = 16
CONV_DIM = D_MODEL // 2
ATTN_HEADS = 16
HEAD_DIM = 64
ATTN_DIM = ATTN_HEADS * HEAD_DIM
PROJ_DIM = 3 * CONV_DIM + 3 * ATTN_DIM
D_FF = 4 * D_MODEL
GRID_W = 64
WIN_ROWS = 8
WIN_COLS = 16
RMS_EPS = 1e-6
NEG_INF = -1e30

PANEL = 1024
P_B, P_C, P_U, P_Q, P_K, P_V = range(6)

V7X_VMEM_LIMIT_BYTES = 60 * 1024 * 1024
BF16_SUBLANES = 16
LANES = 128

HEADS_PER_GROUP = 4
GROUP_W = HEADS_PER_GROUP * HEAD_DIM
N_GROUPS = ATTN_HEADS // HEADS_PER_GROUP
ROWS_PER_STEP = 8
WIN_KEYS = WIN_ROWS * GRID_W
META_PAD = LANES
ALL_KEYS = WIN_KEYS + META_PAD
ROW_UNROLL = 8


def _rms(xf, g):
    return xf * lax.rsqrt(jnp.mean(xf * xf, axis=-1, keepdims=True) + RMS_EPS) * g


def _in_proj_first_kernel(x_ref, meta_ref, g_ref, w_ref, o_ref, om_ref, wb_ref, hn_ref):
    tm = x_ref.shape[0]

    @pl.when(pl.program_id(0) == 0)
    def _():
        hn_ref[0:tm, :] = _rms(x_ref[...], g_ref[...]).astype(jnp.bfloat16)
        hn_ref[tm:, :] = _rms(meta_ref[...], g_ref[...]).astype(jnp.bfloat16)

    w = w_ref[...].astype(jnp.bfloat16)
    wb_ref[...] = w
    r = jnp.dot(hn_ref[...], w, preferred_element_type=jnp.float32)
    o_ref[...] = r[0:tm].astype(o_ref.dtype)
    om_ref[...] = r[tm:].astype(om_ref.dtype)


def _in_proj_rest_kernel(x_ref, g_ref, w_ref, first_ref, wo_ref, o_ref, wob_ref, hn_ref):
    i = pl.program_id(0)

    @pl.when(i == 0)
    def _():
        o_ref[...] = first_ref[...]

    @pl.when(i > 0)
    def _():
        @pl.when(pl.program_id(1) == 0)
        def _():
            hn_ref[...] = _rms(x_ref[...], g_ref[...]).astype(jnp.bfloat16)

        for p in range(w_ref.shape[0]):
            o_ref[p] = jnp.dot(hn_ref[...], w_ref[p],
                               preferred_element_type=jnp.float32).astype(o_ref.dtype)

    wob_ref[...] = wo_ref[...].astype(jnp.bfloat16)


CAST_STEPS = 16
PANELS_PER_STEP = 2


def _in_proj(x2d, meta, g, w_f32, w_out, tm):
    rows = x2d.shape[0]
    nj = PROJ_DIM // PANEL
    ni = rows // tm
    pp = PANELS_PER_STEP
    njr = nj // pp
    assert ni * njr >= CAST_STEPS
    slab = lambda i, j: jnp.minimum(i * njr + j, CAST_STEPS - 1)
    out_spec = pl.BlockSpec((D_MODEL // CAST_STEPS, D_MODEL), lambda i, j: (slab(i, j), 0))
    params = lambda sem: pltpu.CompilerParams(
        dimension_semantics=sem, vmem_limit_bytes=V7X_VMEM_LIMIT_BYTES)
    proj0, proj_meta, w_bf16 = pl.pallas_call(
        _in_proj_first_kernel,
        grid=(nj,),
        in_specs=[
            pl.BlockSpec((tm, D_MODEL), lambda j: (0, 0)),
            pl.BlockSpec((N_META, D_MODEL), lambda j: (0, 0)),
            pl.BlockSpec((1, D_MODEL), lambda j: (0, 0)),
            pl.BlockSpec((D_MODEL, PANEL), lambda j: (0, j)),
        ],
        out_specs=[
            pl.BlockSpec((None, tm, PANEL), lambda j: (j, 0, 0)),
            pl.BlockSpec((None, N_META, PANEL), lambda j: (j, 0, 0)),
            pl.BlockSpec((None, D_MODEL, PANEL), lambda j: (j, 0, 0)),
        ],
        out_shape=[
            jax.ShapeDtypeStruct((nj, tm, PANEL), jnp.bfloat16),
            jax.ShapeDtypeStruct((nj, N_META, PANEL), jnp.bfloat16),
            jax.ShapeDtypeStruct((nj, D_MODEL, PANEL), jnp.bfloat16),
        ],
        scratch_shapes=[pltpu.VMEM((tm + N_META, D_MODEL), jnp.bfloat16)],
        compiler_params=params(("arbitrary",)),
        name="in_proj_first",
    )(x2d, meta, g, w_f32)
    proj, w_out_b = pl.pallas_call(
        _in_proj_rest_kernel,
        grid=(ni, njr),
        in_specs=[
            pl.BlockSpec((tm, D_MODEL), lambda i, j: (jnp.maximum(i, 1), 0)),
            pl.BlockSpec((1, D_MODEL), lambda i, j: (0, 0)),
            pl.BlockSpec((pp, D_MODEL, PANEL), lambda i, j: (jnp.where(i == 0, 0, j), 0, 0)),
            pl.BlockSpec((pp, tm, PANEL), lambda i, j: (jnp.where(i == 0, j, njr - 1), 0, 0),
                         pipeline_mode=pl.Buffered(1)),
            out_spec,
        ],
        out_specs=[pl.BlockSpec((pp, tm, PANEL), lambda i, j: (j, i, 0)), out_spec],
        out_shape=[jax.ShapeDtypeStruct((nj, rows, PANEL), jnp.bfloat16),
                   jax.ShapeDtypeStruct(w_out.shape, jnp.bfloat16)],
        scratch_shapes=[pltpu.VMEM((tm, D_MODEL), jnp.bfloat16)],
        compiler_params=params(("arbitrary", "arbitrary")),
        name="in_proj_rest",
    )(x2d, g, w_bf16, proj0, w_out)
    return proj, proj_meta, w_out_b


def _attn_kernel(q_ref, k_ref, v_ref, km_ref, vm_ref, bias_ref, g_ref, o_ref,
                 kmp_ref, vmp_ref):
    grp = pl.program_id(1)

    kmp_ref[...] = jnp.zeros_like(kmp_ref)
    vmp_ref[...] = jnp.zeros_like(vmp_ref)
    kmp_ref[0:N_META, :] = km_ref[...]
    vmp_ref[0:N_META, :] = vm_ref[...]

    row_head = lax.broadcasted_iota(jnp.int32, (GROUP_W, GROUP_W), 0) // HEAD_DIM
    col_head = lax.broadcasted_iota(jnp.int32, (GROUP_W, GROUP_W), 1) // HEAD_DIM
    diag = row_head == col_head
    lane = lax.broadcasted_iota(jnp.int32, (GRID_W, LANES), 1)
    low_half = lane < HEAD_DIM
    meta_mask = jnp.where(lane < N_META, 0.0, NEG_INF)
    scale = HEAD_DIM ** -0.5
    nt = (((1,), (1,)), ((), ()))

    def row_body(rl, carry):
        r = grp * ROWS_PER_STEP + rl
        rs = jnp.clip(r - WIN_ROWS // 2, 0, GRID_W - WIN_ROWS)
        case = r - rs
        q0 = pl.multiple_of(rl * GRID_W, GRID_W)
        k0 = pl.multiple_of(rs * GRID_W, GRID_W)
        outs = []
        for hg in range(N_GROUPS):
            lanes = slice(hg * GROUP_W, (hg + 1) * GROUP_W)
            qg = q_ref[pl.ds(q0, GRID_W), lanes] * scale
            qbd = jnp.where(diag, jnp.concatenate([qg] * HEADS_PER_GROUP, axis=0),
                            jnp.zeros((), jnp.bfloat16))
            kx = jnp.concatenate(
                [k_ref[pl.ds(k0, WIN_KEYS), lanes], kmp_ref[:, lanes]], axis=0)
            vx = jnp.concatenate(
                [v_ref[pl.ds(k0, WIN_KEYS), lanes], vmp_ref[:, lanes]], axis=0)
            s = lax.dot_general(qbd, kx, nt, preferred_element_type=jnp.float32)
            bias = jnp.concatenate([
                jnp.concatenate(
                    [bias_ref[hg * HEADS_PER_GROUP + h, 2 * p + (WIN_ROWS - 1) - case]
                     for p in range(WIN_ROWS // 2)] + [meta_mask], axis=1)
                for h in range(HEADS_PER_GROUP)], axis=0)
            s = s + bias
            p = jnp.exp(s - jnp.max(s, axis=-1, keepdims=True))
            inv_l = 1.0 / jnp.sum(p, axis=-1, keepdims=True)
            o_all = jnp.dot(p.astype(jnp.bfloat16), vx,
                            preferred_element_type=jnp.float32)
            tiles = []
            for t in range(GROUP_W // LANES):
                ra = slice(2 * t * GRID_W, (2 * t + 1) * GRID_W)
                rb = slice((2 * t + 1) * GRID_W, (2 * t + 2) * GRID_W)
                lt = slice(t * LANES, (t + 1) * LANES)
                tiles.append(jnp.where(low_half, o_all[ra, lt] * inv_l[ra],
                                       o_all[rb, lt] * inv_l[rb]))
            outs.append(jnp.concatenate(tiles, axis=1))
        ssq = sum(jnp.sum(o * o, axis=-1, keepdims=True) for o in outs)
        inv = lax.rsqrt(ssq / ATTN_DIM + RMS_EPS)
        for hg in range(N_GROUPS):
            lanes = slice(hg * GROUP_W, (hg + 1) * GROUP_W)
            o_ref[pl.ds(q0, GRID_W), lanes] = (
                outs[hg] * inv * g_ref[:, lanes]).astype(o_ref.dtype)
        return carry

    lax.fori_loop(0, ROWS_PER_STEP, row_body, 0, unroll=ROW_UNROLL)


def _attention(proj, proj_meta, bias, g, batch, seq):
    rows = seq // GRID_W
    steps = rows // ROWS_PER_STEP
    tq = ROWS_PER_STEP * GRID_W
    single = pl.Buffered(1)
    return pl.pallas_call(
        _attn_kernel,
        grid=(batch, steps),
        in_specs=[
            pl.BlockSpec((None, tq, PANEL), lambda b, s: (P_Q, b * steps + s, 0)),
            pl.BlockSpec((None, seq, PANEL), lambda b, s: (P_K, b, 0), pipeline_mode=single),
            pl.BlockSpec((None, seq, PANEL), lambda b, s: (P_V, b, 0), pipeline_mode=single),
            pl.BlockSpec((None, N_META, PANEL), lambda b, s: (P_K, 0, 0)),
            pl.BlockSpec((None, N_META, PANEL), lambda b, s: (P_V, 0, 0)),
            pl.BlockSpec(bias.shape, lambda b, s: (0, 0, 0, 0), pipeline_mode=single),
            pl.BlockSpec((1, ATTN_DIM), lambda b, s: (0, 0)),
        ],
        out_specs=pl.BlockSpec((tq, ATTN_DIM), lambda b, s: (b * steps + s, 0)),
        out_shape=jax.ShapeDtypeStruct((batch * seq, ATTN_DIM), jnp.bfloat16),
        scratch_shapes=[pltpu.VMEM((META_PAD, PANEL), jnp.bfloat16),
                        pltpu.VMEM((META_PAD, PANEL), jnp.bfloat16)],
        compiler_params=pltpu.CompilerParams(
            dimension_semantics=("arbitrary", "arbitrary"),
            vmem_limit_bytes=V7X_VMEM_LIMIT_BYTES),
        name="attention",
    )(proj, proj, proj, proj_meta, proj_meta, bias, g)


N_DR = 2 * WIN_ROWS - 1
N_DC = 2 * WIN_COLS - 1


N_PAIR = N_DR - 1


def _bias_kernel(rpb_ref, o_ref):
    shape = (GRID_W, LANES)
    qc = lax.broadcasted_iota(jnp.int32, shape, 0)
    lane = lax.broadcasted_iota(jnp.int32, shape, 1)
    kc = lane % GRID_W
    c0 = jnp.clip(qc - WIN_COLS // 2, 0, GRID_W - WIN_COLS)
    valid = (kc >= c0) & (kc < c0 + WIN_COLS)
    low_half = lane < GRID_W

    def toeplitz(dr, lane0):
        row = jnp.broadcast_to(rpb_ref[0, dr:dr + 1, :], shape)
        return pltpu.roll(row, (lane0 - (WIN_COLS - 1)) % LANES, 1, stride=1, stride_axis=0)

    low = [toeplitz(dr, 0) for dr in range(N_PAIR)]
    high = [toeplitz(dr + 1, GRID_W) for dr in range(N_PAIR)]
    for dr0 in range(N_PAIR):
        o_ref[0, dr0] = jnp.where(valid, jnp.where(low_half, low[dr0], high[dr0]), NEG_INF)


def _bias_table(rpb):
    n_dr_pad = N_DR + 1
    rpb_rows = jnp.pad(rpb, ((0, 0), (0, n_dr_pad - N_DR), (0, LANES - N_DC)))
    return pl.pallas_call(
        _bias_kernel,
        grid=(ATTN_HEADS,),
        in_specs=[pl.BlockSpec((1, n_dr_pad, LANES), lambda h: (h, 0, 0))],
        out_specs=pl.BlockSpec((1, N_PAIR, GRID_W, LANES), lambda h: (h, 0, 0, 0)),
        out_shape=jax.ShapeDtypeStruct((ATTN_HEADS, N_PAIR, GRID_W, LANES), jnp.float32),
        compiler_params=pltpu.CompilerParams(dimension_semantics=("arbitrary",)),
        name="bias_table",
    )(rpb_rows)


def _mix_kernel(b_ref, c_ref, u_ref, cp_ref, up_ref, cn_ref, un_ref, cm_ref, um_ref,
                ya_ref, x_ref, cw_ref, cb_ref, cg_ref, wo_ref, o_ref, *, blocks_per_seq):
    i = pl.program_id(0)
    tm = b_ref.shape[0]
    f32 = jnp.float32
    v = c_ref[...].astype(f32) * u_ref[...].astype(f32)
    last = BF16_SUBLANES - 1
    v_before = cp_ref[last:, :].astype(f32) * up_ref[last:, :].astype(f32)
    v_meta = cm_ref[N_META - 1:, :].astype(f32) * um_ref[N_META - 1:, :].astype(f32)
    v_after = cn_ref[0:1, :].astype(f32) * un_ref[0:1, :].astype(f32)
    seq_first = (i % blocks_per_seq) == 0
    seq_last = (i % blocks_per_seq) == blocks_per_seq - 1
    v_before = jnp.where(seq_first, v_meta, v_before)
    v_after = jnp.where(seq_last, 0.0, v_after)
    t = lax.broadcasted_iota(jnp.int32, (tm, 1), 0)
    v_prev = jnp.where(t == 0, v_before, pltpu.roll(v, 1, axis=0))
    v_next = jnp.where(t == tm - 1, v_after, pltpu.roll(v, tm - 1, axis=0))
    conv = (v_prev * cw_ref[0:1, :] + v * cw_ref[1:2, :] + v_next * cw_ref[2:3, :]
            + cb_ref[...])
    y = b_ref[...].astype(f32) * conv
    yc = _rms(y, cg_ref[...]).astype(jnp.bfloat16)
    acc = jnp.dot(yc, wo_ref[0:CONV_DIM, :], preferred_element_type=f32)
    acc = acc + jnp.dot(ya_ref[...], wo_ref[CONV_DIM:, :], preferred_element_type=f32)
    o_ref[...] = x_ref[...] + acc


def _mix(proj, proj_meta, y_attn, x2d, conv_w, conv_b, conv_g, w_out_bf16, seq, tm):
    rows = x2d.shape[0]
    nblk = rows // tm
    hb = tm // BF16_SUBLANES
    n_hb = rows // BF16_SUBLANES
    panel = lambda p: pl.BlockSpec((None, tm, PANEL), lambda i, p=p: (p, i, 0))
    prev = lambda p: pl.BlockSpec((None, BF16_SUBLANES, PANEL),
                                  lambda i, p=p: (p, jnp.maximum(i * hb - 1, 0), 0))
    nxt = lambda p: pl.BlockSpec((None, BF16_SUBLANES, PANEL),
                                 lambda i, p=p: (p, jnp.minimum((i + 1) * hb, n_hb - 1), 0))
    meta = lambda p: pl.BlockSpec((None, N_META, PANEL), lambda i, p=p: (p, 0, 0))
    vec = lambda n: pl.BlockSpec((n, CONV_DIM), lambda i: (0, 0))
    return pl.pallas_call(
        functools.partial(_mix_kernel, blocks_per_seq=seq // tm),
        grid=(nblk,),
        in_specs=[
            panel(P_B), panel(P_C), panel(P_U),
            prev(P_C), prev(P_U), nxt(P_C), nxt(P_U), meta(P_C), meta(P_U),
            pl.BlockSpec((tm, ATTN_DIM), lambda i: (i, 0)),
            pl.BlockSpec((tm, D_MODEL), lambda i: (i, 0)),
            vec(3), vec(1), vec(1),
            pl.BlockSpec((D_MODEL, D_MODEL), lambda i: (0, 0),
                         pipeline_mode=pl.Buffered(1)),
        ],
        out_specs=pl.BlockSpec((tm, D_MODEL), lambda i: (i, 0)),
        out_shape=jax.ShapeDtypeStruct((rows, D_MODEL), jnp.float32),
        compiler_params=pltpu.CompilerParams(
            dimension_semantics=("arbitrary",),
            vmem_limit_bytes=V7X_VMEM_LIMIT_BYTES),
        name="mix",
    )(proj, proj, proj, proj, proj, proj, proj, proj_meta, proj_meta,
      y_attn, x2d, conv_w, conv_b, conv_g, w_out_bf16)


def _mlp_step(f, nf, h_ref, g2_ref, wu_ref, wd_ref, gf_ref, o_ref, hn_ref):
    @pl.when(f == 0)
    def _():
        h = h_ref[...]
        hn_ref[...] = _rms(h, g2_ref[...]).astype(jnp.bfloat16)
        o_ref[...] = h

    wu = jnp.concatenate([wu_ref[s] for s in range(wu_ref.shape[0])], axis=1)
    a = jnp.dot(hn_ref[...], wu, preferred_element_type=jnp.float32)
    a = jnp.square(jnp.maximum(a, 0.0)).astype(jnp.bfloat16)
    o_ref[...] += jnp.dot(a, wd_ref[...], preferred_element_type=jnp.float32)

    @pl.when(f == nf - 1)
    def _():
        o_ref[...] = _rms(o_ref[...], gf_ref[...])


def _mlp_first_kernel(h_ref, g2_ref, wu_ref, wd_ref, gf_ref, o_ref, wub_ref, wdb_ref, hn_ref):
    slab_w = wub_ref.shape[2]
    for s in range(wub_ref.shape[0]):
        wub_ref[s] = wu_ref[:, s * slab_w:(s + 1) * slab_w].astype(jnp.bfloat16)
    wdb_ref[...] = wd_ref[...].astype(jnp.bfloat16)
    _mlp_step(pl.program_id(0), pl.num_programs(0),
              h_ref, g2_ref, wub_ref, wdb_ref, gf_ref, o_ref, hn_ref)


def _mlp_rest_kernel(h_ref, g2_ref, wu_ref, wd_ref, gf_ref, o_ref, hn_ref):
    _mlp_step(pl.program_id(1), pl.num_programs(1),
              h_ref, g2_ref, wu_ref, wd_ref, gf_ref, o_ref, hn_ref)


UP_SLAB = 256


def _mlp(h1, g2, w_up_f32, w_down_f32, gf, tm, tf, tf_rest):
    rows = h1.shape[0]
    n_slab = tf // UP_SLAB
    params = lambda sem: pltpu.CompilerParams(
        dimension_semantics=sem, vmem_limit_bytes=V7X_VMEM_LIMIT_BYTES)
    once = pl.Buffered(1)
    out, w_up_b, w_down_b = pl.pallas_call(
        _mlp_first_kernel,
        grid=(D_FF // tf,),
        in_specs=[
            pl.BlockSpec((tm, D_MODEL), lambda f: (0, 0), pipeline_mode=once),
            pl.BlockSpec((1, D_MODEL), lambda f: (0, 0)),
            pl.BlockSpec((D_MODEL, tf), lambda f: (0, f)),
            pl.BlockSpec((tf, D_MODEL), lambda f: (f, 0)),
            pl.BlockSpec((1, D_MODEL), lambda f: (0, 0)),
        ],
        out_specs=[
            pl.BlockSpec((tm, D_MODEL), lambda f: (0, 0), pipeline_mode=once),
            pl.BlockSpec((n_slab, D_MODEL, UP_SLAB), lambda f: (f, 0, 0)),
            pl.BlockSpec((tf, D_MODEL), lambda f: (f, 0)),
        ],
        out_shape=[
            jax.ShapeDtypeStruct((rows, D_MODEL), jnp.float32),
            jax.ShapeDtypeStruct((D_FF // UP_SLAB, D_MODEL, UP_SLAB), jnp.bfloat16),
            jax.ShapeDtypeStruct((D_FF, D_MODEL), jnp.bfloat16),
        ],
        scratch_shapes=[pltpu.VMEM((tm, D_MODEL), jnp.bfloat16)],
        input_output_aliases={0: 0},
        compiler_params=params(("arbitrary",)),
        name="mlp_first",
    )(h1, g2, w_up_f32, w_down_f32, gf)
    return pl.pallas_call(
        _mlp_rest_kernel,
        grid=(rows // tm - 1, D_FF // tf_rest),
        in_specs=[
            pl.BlockSpec((tm, D_MODEL), lambda i, f: (i + 1, 0)),
            pl.BlockSpec((1, D_MODEL), lambda i, f: (0, 0)),
            pl.BlockSpec((tf_rest // UP_SLAB, D_MODEL, UP_SLAB), lambda i, f: (f, 0, 0)),
            pl.BlockSpec((tf_rest, D_MODEL), lambda i, f: (f, 0)),
            pl.BlockSpec((1, D_MODEL), lambda i, f: (0, 0)),
        ],
        out_specs=pl.BlockSpec((tm, D_MODEL), lambda i, f: (i + 1, 0)),
        out_shape=jax.ShapeDtypeStruct((rows, D_MODEL), jnp.float32),
        scratch_shapes=[pltpu.VMEM((tm, D_MODEL), jnp.bfloat16)],
        input_output_aliases={0: 0},
        compiler_params=params(("arbitrary", "arbitrary")),
        name="mlp_rest",
    )(out, g2, w_up_b, w_down_b, gf)


def kernel(x, meta_tokens, norm1_g, w_in, conv_w, conv_b, conv_norm_g, attn_rpb,
           attn_norm_g, w_out, norm2_g, w_up, w_down, final_norm_g):
    batch, seq, _ = x.shape
    x2d = x.reshape(batch * seq, D_MODEL)
    proj, proj_meta, w_out_b = _in_proj(x2d, meta_tokens, norm1_g, w_in[0], w_out[0], tm=1024)
    bias = _bias_table(attn_rpb[0])
    y_attn = _attention(proj, proj_meta, bias, attn_norm_g, batch, seq)
    h1 = _mix(proj, proj_meta, y_attn, x2d, conv_w[0], conv_b, conv_norm_g, w_out_b,
              seq, tm=512)
    out = _mlp(h1, norm2_g, w_up[0], w_down[0], final_norm_g.reshape(1, D_MODEL),
               tm=1024, tf=512, tf_rest=1024)
    return out.reshape(batch, seq, D_MODEL)
```

```python
import functools

import jax
import jax.numpy as jnp
from jax import lax
from jax.experimental import pallas as pl
from jax.experimental.pallas import tpu as pltpu

D_MODEL = 2048
N_META = 16
CONV_DIM = D_MODEL // 2
ATTN_HEADS = 16
HEAD_DIM = 64
ATTN_DIM = ATTN_HEADS * HEAD_DIM
PROJ_DIM = 3 * CONV_DIM + 3 * ATTN_DIM
D_FF = 4 * D_MODEL
GRID_W = 64
WIN_ROWS = 8
WIN_COLS = 16
RMS_EPS = 1e-6
NEG_INF = -1e30

PANEL = 1024
P_B, P_C, P_U, P_Q, P_K, P_V = range(6)

V7X_VMEM_LIMIT_BYTES = 60 * 1024 * 1024
BF16_SUBLANES = 16
LANES = 128

HEADS_PER_GROUP = 4
GROUP_W = HEADS_PER_GROUP * HEAD_DIM
N_GROUPS = ATTN_HEADS // HEADS_PER_GROUP
ROWS_PER_STEP = 8
WIN_KEYS = WIN_ROWS * GRID_W
META_PAD = LANES
ALL_KEYS = WIN_KEYS + META_PAD
N_DR = 2 * WIN_ROWS - 1
N_DC = 2 * WIN_COLS - 1
N_PAIR = N_DR - 1
ROW_UNROLL = 8


def _rms(xf, g):
    return xf * lax.rsqrt(jnp.mean(xf * xf, axis=-1, keepdims=True) + RMS_EPS) * g


def _in_proj_first_kernel(x_ref, meta_ref, g_ref, w_ref, o_ref, om_ref, wb_ref, hn_ref):
    tm = x_ref.shape[0]

    @pl.when(pl.program_id(0) == 0)
    def _():
        hn_ref[0:tm, :] = _rms(x_ref[...], g_ref[...]).astype(jnp.bfloat16)
        hn_ref[tm:, :] = _rms(meta_ref[...], g_ref[...]).astype(jnp.bfloat16)

    w = w_ref[...].astype(jnp.bfloat16)
    wb_ref[...] = w
    r = jnp.dot(hn_ref[...], w, preferred_element_type=jnp.float32)
    o_ref[...] = r[0:tm].astype(o_ref.dtype)
    om_ref[...] = r[tm:].astype(om_ref.dtype)


def _in_proj_rest_kernel(x_ref, g_ref, w_ref, first_ref, wo_ref, o_ref, wob_ref, hn_ref):
    i = pl.program_id(0)

    @pl.when(i == 0)
    def _():
        o_ref[...] = first_ref[...]

    @pl.when(i > 0)
    def _():
        @pl.when(pl.program_id(1) == 0)
        def _():
            hn_ref[...] = _rms(x_ref[...], g_ref[...]).astype(jnp.bfloat16)

        for p in range(w_ref.shape[0]):
            o_ref[p] = jnp.dot(hn_ref[...], w_ref[p],
                               preferred_element_type=jnp.float32).astype(o_ref.dtype)

    wob_ref[...] = wo_ref[...].astype(jnp.bfloat16)


CAST_STEPS = 16
PANELS_PER_STEP = 2


def _in_proj(x2d, meta, g, w_f32, w_out, tm):
    rows = x2d.shape[0]
    nj = PROJ_DIM // PANEL
    ni = rows // tm
    pp = PANELS_PER_STEP
    njr = nj // pp
    assert ni * njr >= CAST_STEPS
    slab = lambda i, j: jnp.minimum(i * njr + j, CAST_STEPS - 1)
    out_spec = pl.BlockSpec((D_MODEL // CAST_STEPS, D_MODEL), lambda i, j: (slab(i, j), 0))
    params = lambda sem: pltpu.CompilerParams(
        dimension_semantics=sem, vmem_limit_bytes=V7X_VMEM_LIMIT_BYTES)
    proj0, proj_meta, w_bf16 = pl.pallas_call(
        _in_proj_first_kernel,
        grid=(nj,),
        in_specs=[
            pl.BlockSpec((tm, D_MODEL), lambda j: (0, 0)),
            pl.BlockSpec((N_META, D_MODEL), lambda j: (0, 0)),
            pl.BlockSpec((1, D_MODEL), lambda j: (0, 0)),
            pl.BlockSpec((D_MODEL, PANEL), lambda j: (0, j)),
        ],
        out_specs=[
            pl.BlockSpec((None, tm, PANEL), lambda j: (j, 0, 0)),
            pl.BlockSpec((None, N_META, PANEL), lambda j: (j, 0, 0)),
            pl.BlockSpec((None, D_MODEL, PANEL), lambda j: (j, 0, 0)),
        ],
        out_shape=[
            jax.ShapeDtypeStruct((nj, tm, PANEL), jnp.bfloat16),
            jax.ShapeDtypeStruct((nj, N_META, PANEL), jnp.bfloat16),
            jax.ShapeDtypeStruct((nj, D_MODEL, PANEL), jnp.bfloat16),
        ],
        scratch_shapes=[pltpu.VMEM((tm + N_META, D_MODEL), jnp.bfloat16)],
        compiler_params=params(("arbitrary",)),
        name="in_proj_first",
    )(x2d, meta, g, w_f32)
    proj, w_out_b = pl.pallas_call(
        _in_proj_rest_kernel,
        grid=(ni, njr),
        in_specs=[
            pl.BlockSpec((tm, D_MODEL), lambda i, j: (jnp.maximum(i, 1), 0)),
            pl.BlockSpec((1, D_MODEL), lambda i, j: (0, 0)),
            pl.BlockSpec((pp, D_MODEL, PANEL), lambda i, j: (jnp.where(i == 0, 0, j), 0, 0)),
            pl.BlockSpec((pp, tm, PANEL), lambda i, j: (jnp.where(i == 0, j, njr - 1), 0, 0),
                         pipeline_mode=pl.Buffered(1)),
            out_spec,
        ],
        out_specs=[pl.BlockSpec((pp, tm, PANEL), lambda i, j: (j, i, 0)), out_spec],
        out_shape=[jax.ShapeDtypeStruct((nj, rows, PANEL), jnp.bfloat16),
                   jax.ShapeDtypeStruct(w_out.shape, jnp.bfloat16)],
        scratch_shapes=[pltpu.VMEM((tm, D_MODEL), jnp.bfloat16)],
        compiler_params=params(("arbitrary", "arbitrary")),
        name="in_proj_rest",
    )(x2d, g, w_bf16, proj0, w_out)
    return proj, proj_meta, w_out_b


def _bias_tiles(rpb_ref, bias_ref):
    shape = (GRID_W, LANES)
    qc = lax.broadcasted_iota(jnp.int32, shape, 0)
    lane = lax.broadcasted_iota(jnp.int32, shape, 1)
    kc = lane % GRID_W
    c0 = jnp.clip(qc - WIN_COLS // 2, 0, GRID_W - WIN_COLS)
    valid = (kc >= c0) & (kc < c0 + WIN_COLS)
    low_half = lane < GRID_W

    def head_body(h, carry):
        def toeplitz(dr, lane0):
            row = jnp.broadcast_to(rpb_ref[h, dr:dr + 1, :], shape)
            return pltpu.roll(row, (lane0 - (WIN_COLS - 1)) % LANES, 1, stride=1, stride_axis=0)

        for dr0 in range(N_PAIR):
            tile = jnp.where(low_half, toeplitz(dr0, 0), toeplitz(dr0 + 1, GRID_W))
            bias_ref[h, dr0] = jnp.where(valid, tile, NEG_INF)
        return carry

    lax.fori_loop(0, ATTN_HEADS, head_body, 0)


def _attn_kernel(q_ref, k_ref, v_ref, km_ref, vm_ref, rpb_ref, g_ref, o_ref,
                 kmp_ref, vmp_ref, bias_ref):
    grp = pl.program_id(1)

    @pl.when(jnp.logical_and(pl.program_id(0) == 0, grp == 0))
    def _():
        _bias_tiles(rpb_ref, bias_ref)

    kmp_ref[...] = jnp.zeros_like(kmp_ref)
    vmp_ref[...] = jnp.zeros_like(vmp_ref)
    kmp_ref[0:N_META, :] = km_ref[...]
    vmp_ref[0:N_META, :] = vm_ref[...]

    row_head = lax.broadcasted_iota(jnp.int32, (GROUP_W, GROUP_W), 0) // HEAD_DIM
    col_head = lax.broadcasted_iota(jnp.int32, (GROUP_W, GROUP_W), 1) // HEAD_DIM
    diag = row_head == col_head
    lane = lax.broadcasted_iota(jnp.int32, (GRID_W, LANES), 1)
    low_half = lane < HEAD_DIM
    meta_mask = jnp.where(lane < N_META, 0.0, NEG_INF)
    scale = HEAD_DIM ** -0.5
    nt = (((1,), (1,)), ((), ()))

    def row_body(rl, carry):
        r = grp * ROWS_PER_STEP + rl
        rs = jnp.clip(r - WIN_ROWS // 2, 0, GRID_W - WIN_ROWS)
        case = r - rs
        q0 = pl.multiple_of(rl * GRID_W, GRID_W)
        k0 = pl.multiple_of(rs * GRID_W, GRID_W)
        outs = []
        for hg in range(N_GROUPS):
            lanes = slice(hg * GROUP_W, (hg + 1) * GROUP_W)
            qg = q_ref[pl.ds(q0, GRID_W), lanes] * scale
            qbd = jnp.where(diag, jnp.concatenate([qg] * HEADS_PER_GROUP, axis=0),
                            jnp.zeros((), jnp.bfloat16))
            kx = jnp.concatenate(
                [k_ref[pl.ds(k0, WIN_KEYS), lanes], kmp_ref[:, lanes]], axis=0)
            vx = jnp.concatenate(
                [v_ref[pl.ds(k0, WIN_KEYS), lanes], vmp_ref[:, lanes]], axis=0)
            s = lax.dot_general(qbd, kx, nt, preferred_element_type=jnp.float32)
            bias = jnp.concatenate([
                jnp.concatenate(
                    [bias_ref[hg * HEADS_PER_GROUP + h, 2 * p + (WIN_ROWS - 1) - case]
                     for p in range(WIN_ROWS // 2)] + [meta_mask], axis=1)
                for h in range(HEADS_PER_GROUP)], axis=0)
            s = s + bias
            p = jnp.exp(s - jnp.max(s, axis=-1, keepdims=True))
            inv_l = 1.0 / jnp.sum(p, axis=-1, keepdims=True)
            o_all = jnp.dot(p.astype(jnp.bfloat16), vx,
                            preferred_element_type=jnp.float32)
            tiles = []
            for t in range(GROUP_W // LANES):
                ra = slice(2 * t * GRID_W, (2 * t + 1) * GRID_W)
                rb = slice((2 * t + 1) * GRID_W, (2 * t + 2) * GRID_W)
                lt = slice(t * LANES, (t + 1) * LANES)
                tiles.append(jnp.where(low_half, o_all[ra, lt] * inv_l[ra],
                                       o_all[rb, lt] * inv_l[rb]))
            outs.append(jnp.concatenate(tiles, axis=1))
        ssq = sum(jnp.sum(o * o, axis=-1, keepdims=True) for o in outs)
        inv = lax.rsqrt(ssq / ATTN_DIM + RMS_EPS)
        for hg in range(N_GROUPS):
            lanes = slice(hg * GROUP_W, (hg + 1) * GROUP_W)
            o_ref[pl.ds(q0, GRID_W), lanes] = (
                outs[hg] * inv * g_ref[:, lanes]).astype(o_ref.dtype)
        return carry

    lax.fori_loop(0, ROWS_PER_STEP, row_body, 0, unroll=ROW_UNROLL)


def _attention(proj, proj_meta, rpb, g, batch, seq):
    rows = seq // GRID_W
    steps = rows // ROWS_PER_STEP
    tq = ROWS_PER_STEP * GRID_W
    single = pl.Buffered(1)
    rpb_rows = jnp.pad(rpb, ((0, 0), (0, 1), (0, LANES - N_DC)))
    return pl.pallas_call(
        _attn_kernel,
        grid=(batch, steps),
        in_specs=[
            pl.BlockSpec((None, tq, PANEL), lambda b, s: (P_Q, b * steps + s, 0)),
            pl.BlockSpec((None, seq, PANEL), lambda b, s: (P_K, b, 0), pipeline_mode=single),
            pl.BlockSpec((None, seq, PANEL), lambda b, s: (P_V, b, 0), pipeline_mode=single),
            pl.BlockSpec((None, N_META, PANEL), lambda b, s: (P_K, 0, 0)),
            pl.BlockSpec((None, N_META, PANEL), lambda b, s: (P_V, 0, 0)),
            pl.BlockSpec(rpb_rows.shape, lambda b, s: (0, 0, 0)),
            pl.BlockSpec((1, ATTN_DIM), lambda b, s: (0, 0)),
        ],
        out_specs=pl.BlockSpec((tq, ATTN_DIM), lambda b, s: (b * steps + s, 0)),
        out_shape=jax.ShapeDtypeStruct((batch * seq, ATTN_DIM), jnp.bfloat16),
        scratch_shapes=[pltpu.VMEM((META_PAD, PANEL), jnp.bfloat16),
                        pltpu.VMEM((META_PAD, PANEL), jnp.bfloat16),
                        pltpu.VMEM((ATTN_HEADS, N_PAIR, GRID_W, LANES), jnp.float32)],
        compiler_params=pltpu.CompilerParams(
            dimension_semantics=("arbitrary", "arbitrary"),
            vmem_limit_bytes=V7X_VMEM_LIMIT_BYTES),
        name="attention",
    )(proj, proj, proj, proj_meta, proj_meta, rpb_rows, g)


def _mix_kernel(b_ref, c_ref, u_ref, cp_ref, up_ref, cn_ref, un_ref, cm_ref, um_ref,
                ya_ref, x_ref, cw_ref, cb_ref, cg_ref, wo_ref, o_ref, *, blocks_per_seq):
    i = pl.program_id(0)
    tm = b_ref.shape[0]
    f32 = jnp.float32
    v = c_ref[...].astype(f32) * u_ref[...].astype(f32)
    last = BF16_SUBLANES - 1
    v_before = cp_ref[last:, :].astype(f32) * up_ref[last:, :].astype(f32)
    v_meta = cm_ref[N_META - 1:, :].astype(f32) * um_ref[N_META - 1:, :].astype(f32)
    v_after = cn_ref[0:1, :].astype(f32) * un_ref[0:1, :].astype(f32)
    seq_first = (i % blocks_per_seq) == 0
    seq_last = (i % blocks_per_seq) == blocks_per_seq - 1
    v_before = jnp.where(seq_first, v_meta, v_before)
    v_after = jnp.where(seq_last, 0.0, v_after)
    t = lax.broadcasted_iota(jnp.int32, (tm, 1), 0)
    v_prev = jnp.where(t == 0, v_before, pltpu.roll(v, 1, axis=0))
    v_next = jnp.where(t == tm - 1, v_after, pltpu.roll(v, tm - 1, axis=0))
    conv = (v_prev * cw_ref[0:1, :] + v * cw_ref[1:2, :] + v_next * cw_ref[2:3, :]
            + cb_ref[...])
    y = b_ref[...].astype(f32) * conv
    yc = _rms(y, cg_ref[...]).astype(jnp.bfloat16)
    acc = jnp.dot(yc, wo_ref[0:CONV_DIM, :], preferred_element_type=f32)
    acc = acc + jnp.dot(ya_ref[...], wo_ref[CONV_DIM:, :], preferred_element_type=f32)
    o_ref[...] = x_ref[...] + acc


def _mix(proj, proj_meta, y_attn, x2d, conv_w, conv_b, conv_g, w_out_bf16, seq, tm):
    rows = x2d.shape[0]
    nblk = rows // tm
    hb = tm // BF16_SUBLANES
    n_hb = rows // BF16_SUBLANES
    panel = lambda p: pl.BlockSpec((None, tm, PANEL), lambda i, p=p: (p, i, 0))
    prev = lambda p: pl.BlockSpec((None, BF16_SUBLANES, PANEL),
                                  lambda i, p=p: (p, jnp.maximum(i * hb - 1, 0), 0))
    nxt = lambda p: pl.BlockSpec((None, BF16_SUBLANES, PANEL),
                                 lambda i, p=p: (p, jnp.minimum((i + 1) * hb, n_hb - 1), 0))
    meta = lambda p: pl.BlockSpec((None, N_META, PANEL), lambda i, p=p: (p, 0, 0))
    vec = lambda n: pl.BlockSpec((n, CONV_DIM), lambda i: (0, 0))
    return pl.pallas_call(
        functools.partial(_mix_kernel, blocks_per_seq=seq // tm),
        grid=(nblk,),
        in_specs=[
            panel(P_B), panel(P_C), panel(P_U),
            prev(P_C), prev(P_U), nxt(P_C), nxt(P_U), meta(P_C), meta(P_U),
            pl.BlockSpec((tm, ATTN_DIM), lambda i: (i, 0)),
            pl.BlockSpec((tm, D_MODEL), lambda i: (i, 0)),
            vec(3), vec(1), vec(1),
            pl.BlockSpec((D_MODEL, D_MODEL), lambda i: (0, 0),
                         pipeline_mode=pl.Buffered(1)),
        ],
        out_specs=pl.BlockSpec((tm, D_MODEL), lambda i: (i, 0)),
        out_shape=jax.ShapeDtypeStruct((rows, D_MODEL), jnp.float32),
        compiler_params=pltpu.CompilerParams(
            dimension_semantics=("arbitrary",),
            vmem_limit_bytes=V7X_VMEM_LIMIT_BYTES),
        name="mix",
    )(proj, proj, proj, proj, proj, proj, proj, proj_meta, proj_meta,
      y_attn, x2d, conv_w, conv_b, conv_g, w_out_bf16)


def _mlp_step(f, nf, h_ref, g2_ref, wu_ref, wd_ref, gf_ref, o_ref, hn_ref):
    @pl.when(f == 0)
    def _():
        h = h_ref[...]
        hn_ref[...] = _rms(h, g2_ref[...]).astype(jnp.bfloat16)
        o_ref[...] = h

    wu = jnp.concatenate([wu_ref[s] for s in range(wu_ref.shape[0])], axis=1)
    a = jnp.dot(hn_ref[...], wu, preferred_element_type=jnp.float32)
    a = jnp.square(jnp.maximum(a, 0.0)).astype(jnp.bfloat16)
    o_ref[...] += jnp.dot(a, wd_ref[...], preferred_element_type=jnp.float32)

    @pl.when(f == nf - 1)
    def _():
        o_ref[...] = _rms(o_ref[...], gf_ref[...])


def _mlp_first_kernel(h_ref, g2_ref, wu_ref, wd_ref, gf_ref, o_ref, wub_ref, wdb_ref, hn_ref):
    slab_w = wub_ref.shape[2]
    for s in range(wub_ref.shape[0]):
        wub_ref[s] = wu_ref[:, s * slab_w:(s + 1) * slab_w].astype(jnp.bfloat16)
    wdb_ref[...] = wd_ref[...].astype(jnp.bfloat16)
    _mlp_step(pl.program_id(0), pl.num_programs(0),
              h_ref, g2_ref, wub_ref, wdb_ref, gf_ref, o_ref, hn_ref)


def _mlp_rest_kernel(h_ref, g2_ref, wu_ref, wd_ref, gf_ref, o_ref, hn_ref):
    _mlp_step(pl.program_id(1), pl.num_programs(1),
              h_ref, g2_ref, wu_ref, wd_ref, gf_ref, o_ref, hn_ref)


UP_SLAB = 256


def _mlp(h1, g2, w_up_f32, w_down_f32, gf, tm, tf, tf_rest):
    rows = h1.shape[0]
    n_slab = tf // UP_SLAB
    params = lambda sem: pltpu.CompilerParams(
        dimension_semantics=sem, vmem_limit_bytes=V7X_VMEM_LIMIT_BYTES)
    once = pl.Buffered(1)
    out, w_up_b, w_down_b = pl.pallas_call(
        _mlp_first_kernel,
        grid=(D_FF // tf,),
        in_specs=[
            pl.BlockSpec((tm, D_MODEL), lambda f: (0, 0), pipeline_mode=once),
            pl.BlockSpec((1, D_MODEL), lambda f: (0, 0)),
            pl.BlockSpec((D_MODEL, tf), lambda f: (0, f)),
            pl.BlockSpec((tf, D_MODEL), lambda f: (f, 0)),
            pl.BlockSpec((1, D_MODEL), lambda f: (0, 0)),
        ],
        out_specs=[
            pl.BlockSpec((tm, D_MODEL), lambda f: (0, 0), pipeline_mode=once),
            pl.BlockSpec((n_slab, D_MODEL, UP_SLAB), lambda f: (f, 0, 0)),
            pl.BlockSpec((tf, D_MODEL), lambda f: (f, 0)),
        ],
        out_shape=[
            jax.ShapeDtypeStruct((rows, D_MODEL), jnp.float32),
            jax.ShapeDtypeStruct((D_FF // UP_SLAB, D_MODEL, UP_SLAB), jnp.bfloat16),
            jax.ShapeDtypeStruct((D_FF, D_MODEL), jnp.bfloat16),
        ],
        scratch_shapes=[pltpu.VMEM((tm, D_MODEL), jnp.bfloat16)],
        input_output_aliases={0: 0},
        compiler_params=params(("arbitrary",)),
        name="mlp_first",
    )(h1, g2, w_up_f32, w_down_f32, gf)
    return pl.pallas_call(
        _mlp_rest_kernel,
        grid=(rows // tm - 1, D_FF // tf_rest),
        in_specs=[
            pl.BlockSpec((tm, D_MODEL), lambda i, f: (i + 1, 0)),
            pl.BlockSpec((1, D_MODEL), lambda i, f: (0, 0)),
            pl.BlockSpec((tf_rest // UP_SLAB, D_MODEL, UP_SLAB), lambda i, f: (f, 0, 0)),
            pl.BlockSpec((tf_rest, D_MODEL), lambda i, f: (f, 0)),
            pl.BlockSpec((1, D_MODEL), lambda i, f: (0, 0)),
        ],
        out_specs=pl.BlockSpec((tm, D_MODEL), lambda i, f: (i + 1, 0)),
        out_shape=jax.ShapeDtypeStruct((rows, D_MODEL), jnp.float32),
        scratch_shapes=[pltpu.VMEM((tm, D_MODEL), jnp.bfloat16)],
        input_output_aliases={0: 0},
        compiler_params=params(("arbitrary", "arbitrary")),
        name="mlp_rest",
    )(out, g2, w_up_b, w_down_b, gf)


def kernel(x, meta_tokens, norm1_g, w_in, conv_w, conv_b, conv_norm_g, attn_rpb,
           attn_norm_g, w_out, norm2_g, w_up, w_down, final_norm_g):
    batch, seq, _ = x.shape
    x2d = x.reshape(batch * seq, D_MODEL)
    proj, proj_meta, w_out_b = _in_proj(x2d, meta_tokens, norm1_g, w_in[0], w_out[0], tm=1024)
    y_attn = _attention(proj, proj_meta, attn_rpb[0], attn_norm_g, batch, seq)
    h1 = _mix(proj, proj_meta, y_attn, x2d, conv_w[0], conv_b, conv_norm_g, w_out_b,
              seq, tm=512)
    out = _mlp(h1, norm2_g, w_up[0], w_down[0], final_norm_g.reshape(1, D_MODEL),
               tm=1024, tf=512, tf_rest=1024)
    return out.reshape(batch, seq, D_MODEL)
```

```python
import functools

import jax
import jax.numpy as jnp
from jax import lax
from jax.experimental import pallas as pl
from jax.experimental.pallas import tpu as pltpu

D_MODEL = 2048
N_META = 16
CONV_DIM = D_MODEL // 2
ATTN_HEADS = 16
HEAD_DIM = 64
ATTN_DIM = ATTN_HEADS * HEAD_DIM
PROJ_DIM = 3 * CONV_DIM + 3 * ATTN_DIM
D_FF = 4 * D_MODEL
GRID_W = 64
WIN_ROWS = 8
WIN_COLS = 16
RMS_EPS = 1e-6
NEG_INF = -1e30

PANEL = 1024
P_B, P_C, P_U, P_Q, P_K, P_V = range(6)

V7X_VMEM_LIMIT_BYTES = 60 * 1024 * 1024
BF16_SUBLANES = 16
LANES = 128

HEADS_PER_GROUP = 4
GROUP_W = HEADS_PER_GROUP * HEAD_DIM
N_GROUPS = ATTN_HEADS // HEADS_PER_GROUP
ROWS_PER_STEP = 8
WIN_KEYS = WIN_ROWS * GRID_W
META_PAD = LANES
ALL_KEYS = WIN_KEYS + META_PAD
N_DR = 2 * WIN_ROWS - 1
N_DC = 2 * WIN_COLS - 1
N_PAIR = N_DR - 1
ROW_UNROLL = 8


def _rms(xf, g):
    return xf * lax.rsqrt(jnp.mean(xf * xf, axis=-1, keepdims=True) + RMS_EPS) * g


LOG2E = 1.4426950408889634
Q_SCALE = HEAD_DIM ** -0.5 * LOG2E


def _panel_scale(panel):
    return jnp.where(panel == P_Q, Q_SCALE, 1.0).astype(jnp.float32)


def _in_proj_first_kernel(x_ref, meta_ref, g_ref, w_ref, o_ref, om_ref, wb_ref, hn_ref):
    tm = x_ref.shape[0]

    @pl.when(pl.program_id(0) == 0)
    def _():
        hn_ref[0:tm, :] = _rms(x_ref[...], g_ref[...]).astype(jnp.bfloat16)
        hn_ref[tm:, :] = _rms(meta_ref[...], g_ref[...]).astype(jnp.bfloat16)

    w = w_ref[...].astype(jnp.bfloat16)
    wb_ref[...] = w
    r = jnp.dot(hn_ref[...], w, preferred_element_type=jnp.float32)
    o_ref[...] = (r[0:tm] * _panel_scale(pl.program_id(0))).astype(o_ref.dtype)
    om_ref[...] = r[tm:].astype(om_ref.dtype)


def _in_proj_rest_kernel(x_ref, g_ref, w_ref, first_ref, wo_ref, o_ref, wob_ref, hn_ref):
    i = pl.program_id(0)

    @pl.when(i == 0)
    def _():
        o_ref[...] = first_ref[...]

    @pl.when(i > 0)
    def _():
        @pl.when(pl.program_id(1) == 0)
        def _():
            hn_ref[...] = _rms(x_ref[...], g_ref[...]).astype(jnp.bfloat16)

        for p in range(w_ref.shape[0]):
            r = jnp.dot(hn_ref[...], w_ref[p], preferred_element_type=jnp.float32)
            panel = pl.program_id(1) * w_ref.shape[0] + p
            o_ref[p] = (r * _panel_scale(panel)).astype(o_ref.dtype)

    wob_ref[...] = wo_ref[...].astype(jnp.bfloat16)


CAST_STEPS = 16
PANELS_PER_STEP = 2


def _in_proj(x2d, meta, g, w_f32, w_out, tm):
    rows = x2d.shape[0]
    nj = PROJ_DIM // PANEL
    ni = rows // tm
    pp = PANELS_PER_STEP
    njr = nj // pp
    assert ni * njr >= CAST_STEPS
    slab = lambda i, j: jnp.minimum(i * njr + j, CAST_STEPS - 1)
    out_spec = pl.BlockSpec((D_MODEL // CAST_STEPS, D_MODEL), lambda i, j: (slab(i, j), 0))
    params = lambda sem: pltpu.CompilerParams(
        dimension_semantics=sem, vmem_limit_bytes=V7X_VMEM_LIMIT_BYTES)
    proj0, proj_meta, w_bf16 = pl.pallas_call(
        _in_proj_first_kernel,
        grid=(nj,),
        in_specs=[
            pl.BlockSpec((tm, D_MODEL), lambda j: (0, 0)),
            pl.BlockSpec((N_META, D_MODEL), lambda j: (0, 0)),
            pl.BlockSpec((1, D_MODEL), lambda j: (0, 0)),
            pl.BlockSpec((D_MODEL, PANEL), lambda j: (0, j)),
        ],
        out_specs=[
            pl.BlockSpec((None, tm, PANEL), lambda j: (j, 0, 0)),
            pl.BlockSpec((None, N_META, PANEL), lambda j: (j, 0, 0)),
            pl.BlockSpec((None, D_MODEL, PANEL), lambda j: (j, 0, 0)),
        ],
        out_shape=[
            jax.ShapeDtypeStruct((nj, tm, PANEL), jnp.bfloat16),
            jax.ShapeDtypeStruct((nj, N_META, PANEL), jnp.bfloat16),
            jax.ShapeDtypeStruct((nj, D_MODEL, PANEL), jnp.bfloat16),
        ],
        scratch_shapes=[pltpu.VMEM((tm + N_META, D_MODEL), jnp.bfloat16)],
        compiler_params=params(("arbitrary",)),
        name="in_proj_first",
    )(x2d, meta, g, w_f32)
    proj, w_out_b = pl.pallas_call(
        _in_proj_rest_kernel,
        grid=(ni, njr),
        in_specs=[
            pl.BlockSpec((tm, D_MODEL), lambda i, j: (jnp.maximum(i, 1), 0)),
            pl.BlockSpec((1, D_MODEL), lambda i, j: (0, 0)),
            pl.BlockSpec((pp, D_MODEL, PANEL), lambda i, j: (jnp.where(i == 0, 0, j), 0, 0)),
            pl.BlockSpec((pp, tm, PANEL), lambda i, j: (jnp.where(i == 0, j, njr - 1), 0, 0),
                         pipeline_mode=pl.Buffered(1)),
            out_spec,
        ],
        out_specs=[pl.BlockSpec((pp, tm, PANEL), lambda i, j: (j, i, 0)), out_spec],
        out_shape=[jax.ShapeDtypeStruct((nj, rows, PANEL), jnp.bfloat16),
                   jax.ShapeDtypeStruct(w_out.shape, jnp.bfloat16)],
        scratch_shapes=[pltpu.VMEM((tm, D_MODEL), jnp.bfloat16)],
        compiler_params=params(("arbitrary", "arbitrary")),
        name="in_proj_rest",
    )(x2d, g, w_bf16, proj0, w_out)
    return proj, proj_meta, w_out_b


def _bias_tiles(rpb_ref, bias_ref):
    shape = (GRID_W, LANES)
    qc = lax.broadcasted_iota(jnp.int32, shape, 0)
    lane = lax.broadcasted_iota(jnp.int32, shape, 1)
    kc = lane % GRID_W
    c0 = jnp.clip(qc - WIN_COLS // 2, 0, GRID_W - WIN_COLS)
    valid = (kc >= c0) & (kc < c0 + WIN_COLS)
    low_half = lane < GRID_W

    def head_body(h, carry):
        def toeplitz(dr, lane0):
            row = jnp.broadcast_to(rpb_ref[h, dr:dr + 1, :], shape)
            return pltpu.roll(row, (lane0 - (WIN_COLS - 1)) % LANES, 1, stride=1, stride_axis=0)

        for dr0 in range(N_PAIR):
            tile = jnp.where(low_half, toeplitz(dr0, 0), toeplitz(dr0 + 1, GRID_W))
            bias_ref[h, dr0] = jnp.where(valid, tile * LOG2E, NEG_INF)
        return carry

    lax.fori_loop(0, ATTN_HEADS, head_body, 0)


def _attn_kernel(q_ref, k_ref, v_ref, km_ref, vm_ref, rpb_ref, g_ref, o_ref,
                 kmp_ref, vmp_ref, bias_ref):
    grp = pl.program_id(1)

    @pl.when(jnp.logical_and(pl.program_id(0) == 0, grp == 0))
    def _():
        _bias_tiles(rpb_ref, bias_ref)

    kmp_ref[...] = jnp.zeros_like(kmp_ref)
    vmp_ref[...] = jnp.zeros_like(vmp_ref)
    kmp_ref[0:N_META, :] = km_ref[...]
    vmp_ref[0:N_META, :] = vm_ref[...]

    row_head = lax.broadcasted_iota(jnp.int32, (GROUP_W, GROUP_W), 0) // HEAD_DIM
    col_head = lax.broadcasted_iota(jnp.int32, (GROUP_W, GROUP_W), 1) // HEAD_DIM
    diag = row_head == col_head
    lane = lax.broadcasted_iota(jnp.int32, (GRID_W, LANES), 1)
    low_half = lane < HEAD_DIM
    meta_mask = jnp.where(lane < N_META, 0.0, NEG_INF)
    nt = (((1,), (1,)), ((), ()))

    def row_body(rl, carry):
        r = grp * ROWS_PER_STEP + rl
        rs = jnp.clip(r - WIN_ROWS // 2, 0, GRID_W - WIN_ROWS)
        case = r - rs
        q0 = pl.multiple_of(rl * GRID_W, GRID_W)
        k0 = pl.multiple_of(rs * GRID_W, GRID_W)
        outs = []
        for hg in range(N_GROUPS):
            lanes = slice(hg * GROUP_W, (hg + 1) * GROUP_W)
            qg = q_ref[pl.ds(q0, GRID_W), lanes]
            qbd = jnp.where(diag, jnp.concatenate([qg] * HEADS_PER_GROUP, axis=0),
                            jnp.zeros((), jnp.bfloat16))
            kx = jnp.concatenate(
                [k_ref[pl.ds(k0, WIN_KEYS), lanes], kmp_ref[:, lanes]], axis=0)
            vx = jnp.concatenate(
                [v_ref[pl.ds(k0, WIN_KEYS), lanes], vmp_ref[:, lanes]], axis=0)
            s = lax.dot_general(qbd, kx, nt, preferred_element_type=jnp.float32)
            bias = jnp.concatenate([
                jnp.concatenate(
                    [bias_ref[hg * HEADS_PER_GROUP + h, 2 * p + (WIN_ROWS - 1) - case]
                     for p in range(WIN_ROWS // 2)] + [meta_mask], axis=1)
                for h in range(HEADS_PER_GROUP)], axis=0)
            s = s + bias
            p = jnp.exp2(s - jnp.max(s, axis=-1, keepdims=True))
            inv_l = 1.0 / jnp.sum(p, axis=-1, keepdims=True)
            o_all = jnp.dot(p.astype(jnp.bfloat16), vx,
                            preferred_element_type=jnp.float32)
            tiles = []
            for t in range(GROUP_W // LANES):
                ra = slice(2 * t * GRID_W, (2 * t + 1) * GRID_W)
                rb = slice((2 * t + 1) * GRID_W, (2 * t + 2) * GRID_W)
                lt = slice(t * LANES, (t + 1) * LANES)
                tiles.append(jnp.where(low_half, o_all[ra, lt] * inv_l[ra],
                                       o_all[rb, lt] * inv_l[rb]))
            outs.append(jnp.concatenate(tiles, axis=1))
        ssq = sum(jnp.sum(o * o, axis=-1, keepdims=True) for o in outs)
        inv = lax.rsqrt(ssq / ATTN_DIM + RMS_EPS)
        for hg in range(N_GROUPS):
            lanes = slice(hg * GROUP_W, (hg + 1) * GROUP_W)
            o_ref[pl.ds(q0, GRID_W), lanes] = (
                outs[hg] * inv * g_ref[:, lanes]).astype(o_ref.dtype)
        return carry

    lax.fori_loop(0, ROWS_PER_STEP, row_body, 0, unroll=ROW_UNROLL)


def _attention(proj, proj_meta, rpb, g, batch, seq):
    rows = seq // GRID_W
    steps = rows // ROWS_PER_STEP
    tq = ROWS_PER_STEP * GRID_W
    single = pl.Buffered(1)
    rpb_rows = jnp.pad(rpb, ((0, 0), (0, 1), (0, LANES - N_DC)))
    return pl.pallas_call(
        _attn_kernel,
        grid=(batch, steps),
        in_specs=[
            pl.BlockSpec((None, tq, PANEL), lambda b, s: (P_Q, b * steps + s, 0)),
            pl.BlockSpec((None, seq, PANEL), lambda b, s: (P_K, b, 0), pipeline_mode=single),
            pl.BlockSpec((None, seq, PANEL), lambda b, s: (P_V, b, 0), pipeline_mode=single),
            pl.BlockSpec((None, N_META, PANEL), lambda b, s: (P_K, 0, 0)),
            pl.BlockSpec((None, N_META, PANEL), lambda b, s: (P_V, 0, 0)),
            pl.BlockSpec(rpb_rows.shape, lambda b, s: (0, 0, 0)),
            pl.BlockSpec((1, ATTN_DIM), lambda b, s: (0, 0)),
        ],
        out_specs=pl.BlockSpec((tq, ATTN_DIM), lambda b, s: (b * steps + s, 0)),
        out_shape=jax.ShapeDtypeStruct((batch * seq, ATTN_DIM), jnp.bfloat16),
        scratch_shapes=[pltpu.VMEM((META_PAD, PANEL), jnp.bfloat16),
                        pltpu.VMEM((META_PAD, PANEL), jnp.bfloat16),
                        pltpu.VMEM((ATTN_HEADS, N_PAIR, GRID_W, LANES), jnp.float32)],
        compiler_params=pltpu.CompilerParams(
            dimension_semantics=("arbitrary", "arbitrary"),
            vmem_limit_bytes=V7X_VMEM_LIMIT_BYTES),
        name="attention",
    )(proj, proj, proj, proj_meta, proj_meta, rpb_rows, g)


def _mix_kernel(b_ref, c_ref, u_ref, cp_ref, up_ref, cn_ref, un_ref, cm_ref, um_ref,
                ya_ref, x_ref, cw_ref, cb_ref, cg_ref, wo_ref, o_ref, *, blocks_per_seq):
    i = pl.program_id(0)
    tm = b_ref.shape[0]
    f32 = jnp.float32
    v = c_ref[...].astype(f32) * u_ref[...].astype(f32)
    last = BF16_SUBLANES - 1
    v_before = cp_ref[last:, :].astype(f32) * up_ref[last:, :].astype(f32)
    v_meta = cm_ref[N_META - 1:, :].astype(f32) * um_ref[N_META - 1:, :].astype(f32)
    v_after = cn_ref[0:1, :].astype(f32) * un_ref[0:1, :].astype(f32)
    seq_first = (i % blocks_per_seq) == 0
    seq_last = (i % blocks_per_seq) == blocks_per_seq - 1
    v_before = jnp.where(seq_first, v_meta, v_before)
    v_after = jnp.where(seq_last, 0.0, v_after)
    t = lax.broadcasted_iota(jnp.int32, (tm, 1), 0)
    v_prev = jnp.where(t == 0, v_before, pltpu.roll(v, 1, axis=0))
    v_next = jnp.where(t == tm - 1, v_after, pltpu.roll(v, tm - 1, axis=0))
    conv = (v_prev * cw_ref[0:1, :] + v * cw_ref[1:2, :] + v_next * cw_ref[2:3, :]
            + cb_ref[...])
    y = b_ref[...].astype(f32) * conv
    yc = _rms(y, cg_ref[...]).astype(jnp.bfloat16)
    acc = jnp.dot(yc, wo_ref[0:CONV_DIM, :], preferred_element_type=f32)
    acc = acc + jnp.dot(ya_ref[...], wo_ref[CONV_DIM:, :], preferred_element_type=f32)
    o_ref[...] = x_ref[...] + acc


def _mix(proj, proj_meta, y_attn, x2d, conv_w, conv_b, conv_g, w_out_bf16, seq, tm):
    rows = x2d.shape[0]
    nblk = rows // tm
    hb = tm // BF16_SUBLANES
    n_hb = rows // BF16_SUBLANES
    panel = lambda p: pl.BlockSpec((None, tm, PANEL), lambda i, p=p: (p, i, 0))
    prev = lambda p: pl.BlockSpec((None, BF16_SUBLANES, PANEL),
                                  lambda i, p=p: (p, jnp.maximum(i * hb - 1, 0), 0))
    nxt = lambda p: pl.BlockSpec((None, BF16_SUBLANES, PANEL),
                                 lambda i, p=p: (p, jnp.minimum((i + 1) * hb, n_hb - 1), 0))
    meta = lambda p: pl.BlockSpec((None, N_META, PANEL), lambda i, p=p: (p, 0, 0))
    vec = lambda n: pl.BlockSpec((n, CONV_DIM), lambda i: (0, 0))
    return pl.pallas_call(
        functools.partial(_mix_kernel, blocks_per_seq=seq // tm),
        grid=(nblk,),
        in_specs=[
            panel(P_B), panel(P_C), panel(P_U),
            prev(P_C), prev(P_U), nxt(P_C), nxt(P_U), meta(P_C), meta(P_U),
            pl.BlockSpec((tm, ATTN_DIM), lambda i: (i, 0)),
            pl.BlockSpec((tm, D_MODEL), lambda i: (i, 0)),
            vec(3), vec(1), vec(1),
            pl.BlockSpec((D_MODEL, D_MODEL), lambda i: (0, 0),
                         pipeline_mode=pl.Buffered(1)),
        ],
        out_specs=pl.BlockSpec((tm, D_MODEL), lambda i: (i, 0)),
        out_shape=jax.ShapeDtypeStruct((rows, D_MODEL), jnp.float32),
        compiler_params=pltpu.CompilerParams(
            dimension_semantics=("arbitrary",),
            vmem_limit_bytes=V7X_VMEM_LIMIT_BYTES),
        name="mix",
    )(proj, proj, proj, proj, proj, proj, proj, proj_meta, proj_meta,
      y_attn, x2d, conv_w, conv_b, conv_g, w_out_bf16)


def _mlp_step(f, nf, h_ref, g2_ref, wu_ref, wd_ref, gf_ref, o_ref, hn_ref, prepare=None):
    @pl.when(f == 0)
    def _():
        h = h_ref[...]
        hn_ref[...] = _rms(h, g2_ref[...]).astype(jnp.bfloat16)
        o_ref[...] = h

    if prepare is not None:
        prepare()
    wu = jnp.concatenate([wu_ref[s] for s in range(wu_ref.shape[0])], axis=1)
    a = jnp.dot(hn_ref[...], wu, preferred_element_type=jnp.float32)
    a = jnp.square(jnp.maximum(a, 0.0)).astype(jnp.bfloat16)
    o_ref[...] += jnp.dot(a, wd_ref[...], preferred_element_type=jnp.float32)

    @pl.when(f == nf - 1)
    def _():
        o_ref[...] = _rms(o_ref[...], gf_ref[...])


def _mlp_first_kernel(h_ref, g2_ref, wu_ref, wd_ref, gf_ref, o_ref, wub_ref, wdb_ref, hn_ref):
    def cast_tiles():
        slab_w = wub_ref.shape[2]
        for s in range(wub_ref.shape[0]):
            wub_ref[s] = wu_ref[:, s * slab_w:(s + 1) * slab_w].astype(jnp.bfloat16)
        wdb_ref[...] = wd_ref[...].astype(jnp.bfloat16)

    _mlp_step(pl.program_id(0), pl.num_programs(0),
              h_ref, g2_ref, wub_ref, wdb_ref, gf_ref, o_ref, hn_ref, prepare=cast_tiles)


def _mlp_rest_kernel(h_ref, g2_ref, wu_ref, wd_ref, gf_ref, o_ref, hn_ref):
    _mlp_step(pl.program_id(1), pl.num_programs(1),
              h_ref, g2_ref, wu_ref, wd_ref, gf_ref, o_ref, hn_ref)


UP_SLAB = 256


def _mlp(h1, g2, w_up_f32, w_down_f32, gf, tm, tf, tf_rest):
    rows = h1.shape[0]
    n_slab = tf // UP_SLAB
    params = lambda sem: pltpu.CompilerParams(
        dimension_semantics=sem, vmem_limit_bytes=V7X_VMEM_LIMIT_BYTES)
    once = pl.Buffered(1)
    out, w_up_b, w_down_b = pl.pallas_call(
        _mlp_first_kernel,
        grid=(D_FF // tf,),
        in_specs=[
            pl.BlockSpec((tm, D_MODEL), lambda f: (0, 0), pipeline_mode=once),
            pl.BlockSpec((1, D_MODEL), lambda f: (0, 0)),
            pl.BlockSpec((D_MODEL, tf), lambda f: (0, f)),
            pl.BlockSpec((tf, D_MODEL), lambda f: (f, 0)),
            pl.BlockSpec((1, D_MODEL), lambda f: (0, 0)),
        ],
        out_specs=[
            pl.BlockSpec((tm, D_MODEL), lambda f: (0, 0), pipeline_mode=once),
            pl.BlockSpec((n_slab, D_MODEL, UP_SLAB), lambda f: (f, 0, 0)),
            pl.BlockSpec((tf, D_MODEL), lambda f: (f, 0)),
        ],
        out_shape=[
            jax.ShapeDtypeStruct((rows, D_MODEL), jnp.float32),
            jax.ShapeDtypeStruct((D_FF // UP_SLAB, D_MODEL, UP_SLAB), jnp.bfloat16),
            jax.ShapeDtypeStruct((D_FF, D_MODEL), jnp.bfloat16),
        ],
        scratch_shapes=[pltpu.VMEM((tm, D_MODEL), jnp.bfloat16)],
        input_output_aliases={0: 0},
        compiler_params=params(("arbitrary",)),
        name="mlp_first",
    )(h1, g2, w_up_f32, w_down_f32, gf)
    return pl.pallas_call(
        _mlp_rest_kernel,
        grid=(rows // tm - 1, D_FF // tf_rest),
        in_specs=[
            pl.BlockSpec((tm, D_MODEL), lambda i, f: (i + 1, 0)),
            pl.BlockSpec((1, D_MODEL), lambda i, f: (0, 0)),
            pl.BlockSpec((tf_rest // UP_SLAB, D_MODEL, UP_SLAB), lambda i, f: (f, 0, 0)),
            pl.BlockSpec((tf_rest, D_MODEL), lambda i, f: (f, 0)),
            pl.BlockSpec((1, D_MODEL), lambda i, f: (0, 0)),
        ],
        out_specs=pl.BlockSpec((tm, D_MODEL), lambda i, f: (i + 1, 0)),
        out_shape=jax.ShapeDtypeStruct((rows, D_MODEL), jnp.float32),
        scratch_shapes=[pltpu.VMEM((tm, D_MODEL), jnp.bfloat16)],
        input_output_aliases={0: 0},
        compiler_params=params(("arbitrary", "arbitrary")),
        name="mlp_rest",
    )(out, g2, w_up_b, w_down_b, gf)


def kernel(x, meta_tokens, norm1_g, w_in, conv_w, conv_b, conv_norm_g, attn_rpb,
           attn_norm_g, w_out, norm2_g, w_up, w_down, final_norm_g):
    batch, seq, _ = x.shape
    x2d = x.reshape(batch * seq, D_MODEL)
    proj, proj_meta, w_out_b = _in_proj(x2d, meta_tokens, norm1_g, w_in[0], w_out[0], tm=1024)
    y_attn = _attention(proj, proj_meta, attn_rpb[0], attn_norm_g, batch, seq)
    h1 = _mix(proj, proj_meta, y_attn, x2d, conv_w[0], conv_b, conv_norm_g, w_out_b,
              seq, tm=512)
    out = _mlp(h1, norm2_g, w_up[0], w_down[0], final_norm_g.reshape(1, D_MODEL),
               tm=1024, tf=512, tf_rest=1024)
    return out.reshape(batch, seq, D_MODEL)
```

```python
import functools

import jax
import jax.numpy as jnp
from jax import lax
from jax.experimental import pallas as pl
from jax.experimental.pallas import tpu as pltpu

D_MODEL = 2048
N_META = 16
CONV_DIM = D_MODEL // 2
ATTN_HEADS = 16
HEAD_DIM = 64
ATTN_DIM = ATTN_HEADS * HEAD_DIM
PROJ_DIM = 3 * CONV_DIM + 3 * ATTN_DIM
D_FF = 4 * D_MODEL
GRID_W = 64
WIN_ROWS = 8
WIN_COLS = 16
RMS_EPS = 1e-6
NEG_INF = -1e30

PANEL = 1024
P_B, P_C, P_U, P_Q, P_K, P_V = range(6)

V7X_VMEM_LIMIT_BYTES = 60 * 1024 * 1024
BF16_SUBLANES = 16
LANES = 128

HEADS_PER_GROUP = 4
GROUP_W = HEADS_PER_GROUP * HEAD_DIM
N_GROUPS = ATTN_HEADS // HEADS_PER_GROUP
ROWS_PER_STEP = 8
WIN_KEYS = WIN_ROWS * GRID_W
META_PAD = LANES
ALL_KEYS = WIN_KEYS + META_PAD
N_DR = 2 * WIN_ROWS - 1
N_DC = 2 * WIN_COLS - 1
N_PAIR = N_DR - 1
ROW_UNROLL = 8


def _inv_rms(xf):
    return lax.rsqrt(jnp.mean(xf * xf, axis=-1, keepdims=True) + RMS_EPS)


def _rms(xf, g):
    return xf * _inv_rms(xf) * g


def _split_norm(xf, hn_ref, inv_ref, rows):
    hn_ref[rows, :] = xf.astype(hn_ref.dtype)
    inv_ref[rows, :] = jnp.broadcast_to(_inv_rms(xf), (xf.shape[0], LANES))


def _scale_rows(r, row_scale):
    return jnp.concatenate([r[:, k:k + LANES] * row_scale
                            for k in range(0, r.shape[1], LANES)], axis=1)


LOG2E = 1.4426950408889634
Q_SCALE = HEAD_DIM ** -0.5 * LOG2E


def _panel_scale(panel):
    return jnp.where(panel == P_Q, Q_SCALE, 1.0).astype(jnp.float32)


def _in_proj_first_kernel(x_ref, meta_ref, g_ref, w_ref, o_ref, om_ref, wb_ref,
                          hn_ref, inv_ref):
    tm = x_ref.shape[0]

    @pl.when(pl.program_id(0) == 0)
    def _():
        _split_norm(x_ref[...], hn_ref, inv_ref, slice(0, tm))
        _split_norm(meta_ref[...], hn_ref, inv_ref, slice(tm, tm + N_META))

    w = (w_ref[...] * g_ref[...]).astype(jnp.bfloat16)
    wb_ref[...] = w
    r = jnp.dot(hn_ref[...], w, preferred_element_type=jnp.float32)
    scale = inv_ref[...] * _panel_scale(pl.program_id(0))
    o_ref[...] = _scale_rows(r[0:tm], scale[0:tm]).astype(o_ref.dtype)
    om_ref[...] = _scale_rows(r[tm:], scale[tm:]).astype(om_ref.dtype)


def _in_proj_rest_kernel(x_ref, w_ref, first_ref, wo_ref, go_ref, o_ref, wob_ref,
                         hn_ref, inv_ref):
    i = pl.program_id(0)

    @pl.when(i == 0)
    def _():
        o_ref[...] = first_ref[...]

    @pl.when(i > 0)
    def _():
        @pl.when(pl.program_id(1) == 0)
        def _():
            _split_norm(x_ref[...], hn_ref, inv_ref, slice(None))

        for p in range(w_ref.shape[0]):
            r = jnp.dot(hn_ref[...], w_ref[p], preferred_element_type=jnp.float32)
            panel = pl.program_id(1) * w_ref.shape[0] + p
            o_ref[p] = _scale_rows(r, inv_ref[...] * _panel_scale(panel)).astype(o_ref.dtype)

    wob_ref[...] = (wo_ref[...] * go_ref[...]).astype(jnp.bfloat16)


CAST_STEPS = 16
PANELS_PER_STEP = 2


def _in_proj(x2d, meta, g_col, w_f32, w_out, g_out_col, tm):
    rows = x2d.shape[0]
    nj = PROJ_DIM // PANEL
    ni = rows // tm
    pp = PANELS_PER_STEP
    njr = nj // pp
    assert ni * njr >= CAST_STEPS
    slab = lambda i, j: jnp.minimum(i * njr + j, CAST_STEPS - 1)
    out_spec = pl.BlockSpec((D_MODEL // CAST_STEPS, D_MODEL), lambda i, j: (slab(i, j), 0))
    gout_spec = pl.BlockSpec((D_MODEL // CAST_STEPS, 1), lambda i, j: (slab(i, j), 0))
    params = lambda sem: pltpu.CompilerParams(
        dimension_semantics=sem, vmem_limit_bytes=V7X_VMEM_LIMIT_BYTES)
    proj0, proj_meta, w_bf16 = pl.pallas_call(
        _in_proj_first_kernel,
        grid=(nj,),
        in_specs=[
            pl.BlockSpec((tm, D_MODEL), lambda j: (0, 0)),
            pl.BlockSpec((N_META, D_MODEL), lambda j: (0, 0)),
            pl.BlockSpec((D_MODEL, 1), lambda j: (0, 0)),
            pl.BlockSpec((D_MODEL, PANEL), lambda j: (0, j)),
        ],
        out_specs=[
            pl.BlockSpec((None, tm, PANEL), lambda j: (j, 0, 0)),
            pl.BlockSpec((None, N_META, PANEL), lambda j: (j, 0, 0)),
            pl.BlockSpec((None, D_MODEL, PANEL), lambda j: (j, 0, 0)),
        ],
        out_shape=[
            jax.ShapeDtypeStruct((nj, tm, PANEL), jnp.bfloat16),
            jax.ShapeDtypeStruct((nj, N_META, PANEL), jnp.bfloat16),
            jax.ShapeDtypeStruct((nj, D_MODEL, PANEL), jnp.bfloat16),
        ],
        scratch_shapes=[pltpu.VMEM((tm + N_META, D_MODEL), jnp.bfloat16),
                        pltpu.VMEM((tm + N_META, LANES), jnp.float32)],
        compiler_params=params(("arbitrary",)),
        name="in_proj_first",
    )(x2d, meta, g_col, w_f32)
    proj, w_out_b = pl.pallas_call(
        _in_proj_rest_kernel,
        grid=(ni, njr),
        in_specs=[
            pl.BlockSpec((tm, D_MODEL), lambda i, j: (jnp.maximum(i, 1), 0)),
            pl.BlockSpec((pp, D_MODEL, PANEL), lambda i, j: (jnp.where(i == 0, 0, j), 0, 0)),
            pl.BlockSpec((pp, tm, PANEL), lambda i, j: (jnp.where(i == 0, j, njr - 1), 0, 0),
                         pipeline_mode=pl.Buffered(1)),
            out_spec, gout_spec,
        ],
        out_specs=[pl.BlockSpec((pp, tm, PANEL), lambda i, j: (j, i, 0)), out_spec],
        out_shape=[jax.ShapeDtypeStruct((nj, rows, PANEL), jnp.bfloat16),
                   jax.ShapeDtypeStruct(w_out.shape, jnp.bfloat16)],
        scratch_shapes=[pltpu.VMEM((tm, D_MODEL), jnp.bfloat16),
                        pltpu.VMEM((tm, LANES), jnp.float32)],
        compiler_params=params(("arbitrary", "arbitrary")),
        name="in_proj_rest",
    )(x2d, w_bf16, proj0, w_out, g_out_col)
    return proj, proj_meta, w_out_b


def _bias_tiles(rpb_ref, bias_ref):
    shape = (GRID_W, LANES)
    qc = lax.broadcasted_iota(jnp.int32, shape, 0)
    lane = lax.broadcasted_iota(jnp.int32, shape, 1)
    kc = lane % GRID_W
    c0 = jnp.clip(qc - WIN_COLS // 2, 0, GRID_W - WIN_COLS)
    valid = (kc >= c0) & (kc < c0 + WIN_COLS)
    low_half = lane < GRID_W

    def head_body(h, carry):
        def toeplitz(dr, lane0):
            row = jnp.broadcast_to(rpb_ref[h, dr:dr + 1, :], shape)
            return pltpu.roll(row, (lane0 - (WIN_COLS - 1)) % LANES, 1, stride=1, stride_axis=0)

        for dr0 in range(N_PAIR):
            tile = jnp.where(low_half, toeplitz(dr0, 0), toeplitz(dr0 + 1, GRID_W))
            bias_ref[h, dr0] = jnp.where(valid, tile * LOG2E, NEG_INF)
        return carry

    lax.fori_loop(0, ATTN_HEADS, head_body, 0)


def _attn_kernel(q_ref, k_ref, v_ref, km_ref, vm_ref, rpb_ref, o_ref,
                 kmp_ref, vmp_ref, bias_ref):
    grp = pl.program_id(1)

    @pl.when(jnp.logical_and(pl.program_id(0) == 0, grp == 0))
    def _():
        _bias_tiles(rpb_ref, bias_ref)

    kmp_ref[...] = jnp.zeros_like(kmp_ref)
    vmp_ref[...] = jnp.zeros_like(vmp_ref)
    kmp_ref[0:N_META, :] = km_ref[...]
    vmp_ref[0:N_META, :] = vm_ref[...]

    row_head = lax.broadcasted_iota(jnp.int32, (GROUP_W, GROUP_W), 0) // HEAD_DIM
    col_head = lax.broadcasted_iota(jnp.int32, (GROUP_W, GROUP_W), 1) // HEAD_DIM
    diag = row_head == col_head
    lane = lax.broadcasted_iota(jnp.int32, (GRID_W, LANES), 1)
    low_half = lane < HEAD_DIM
    meta_mask = jnp.where(lane < N_META, 0.0, NEG_INF)
    nt = (((1,), (1,)), ((), ()))

    def row_body(rl, carry):
        r = grp * ROWS_PER_STEP + rl
        rs = jnp.clip(r - WIN_ROWS // 2, 0, GRID_W - WIN_ROWS)
        case = r - rs
        q0 = pl.multiple_of(rl * GRID_W, GRID_W)
        k0 = pl.multiple_of(rs * GRID_W, GRID_W)
        outs = []
        for hg in range(N_GROUPS):
            lanes = slice(hg * GROUP_W, (hg + 1) * GROUP_W)
            qg = q_ref[pl.ds(q0, GRID_W), lanes]
            qbd = jnp.where(diag, jnp.concatenate([qg] * HEADS_PER_GROUP, axis=0),
                            jnp.zeros((), jnp.bfloat16))
            kx = jnp.concatenate(
                [k_ref[pl.ds(k0, WIN_KEYS), lanes], kmp_ref[:, lanes]], axis=0)
            vx = jnp.concatenate(
                [v_ref[pl.ds(k0, WIN_KEYS), lanes], vmp_ref[:, lanes]], axis=0)
            s = lax.dot_general(qbd, kx, nt, preferred_element_type=jnp.float32)
            bias = jnp.concatenate([
                jnp.concatenate(
                    [bias_ref[hg * HEADS_PER_GROUP + h, 2 * p + (WIN_ROWS - 1) - case]
                     for p in range(WIN_ROWS // 2)] + [meta_mask], axis=1)
                for h in range(HEADS_PER_GROUP)], axis=0)
            s = s + bias
            p = jnp.exp2(s - jnp.max(s, axis=-1, keepdims=True))
            inv_l = 1.0 / jnp.sum(p, axis=-1, keepdims=True)
            o_all = jnp.dot(p.astype(jnp.bfloat16), vx,
                            preferred_element_type=jnp.float32)
            tiles = []
            for t in range(GROUP_W // LANES):
                ra = slice(2 * t * GRID_W, (2 * t + 1) * GRID_W)
                rb = slice((2 * t + 1) * GRID_W, (2 * t + 2) * GRID_W)
                lt = slice(t * LANES, (t + 1) * LANES)
                tiles.append(jnp.where(low_half, o_all[ra, lt] * inv_l[ra],
                                       o_all[rb, lt] * inv_l[rb]))
            outs.append(jnp.concatenate(tiles, axis=1))
        ssq = sum(jnp.sum(o * o, axis=-1, keepdims=True) for o in outs)
        inv = lax.rsqrt(ssq / ATTN_DIM + RMS_EPS)
        for hg in range(N_GROUPS):
            lanes = slice(hg * GROUP_W, (hg + 1) * GROUP_W)
            o_ref[pl.ds(q0, GRID_W), lanes] = (
                outs[hg] * inv).astype(o_ref.dtype)
        return carry

    lax.fori_loop(0, ROWS_PER_STEP, row_body, 0, unroll=ROW_UNROLL)


def _attention(proj, proj_meta, rpb, batch, seq):
    rows = seq // GRID_W
    steps = rows // ROWS_PER_STEP
    tq = ROWS_PER_STEP * GRID_W
    single = pl.Buffered(1)
    rpb_rows = jnp.pad(rpb, ((0, 0), (0, 1), (0, LANES - N_DC)))
    return pl.pallas_call(
        _attn_kernel,
        grid=(batch, steps),
        in_specs=[
            pl.BlockSpec((None, tq, PANEL), lambda b, s: (P_Q, b * steps + s, 0)),
            pl.BlockSpec((None, seq, PANEL), lambda b, s: (P_K, b, 0), pipeline_mode=single),
            pl.BlockSpec((None, seq, PANEL), lambda b, s: (P_V, b, 0), pipeline_mode=single),
            pl.BlockSpec((None, N_META, PANEL), lambda b, s: (P_K, 0, 0)),
            pl.BlockSpec((None, N_META, PANEL), lambda b, s: (P_V, 0, 0)),
            pl.BlockSpec(rpb_rows.shape, lambda b, s: (0, 0, 0)),
        ],
        out_specs=pl.BlockSpec((tq, ATTN_DIM), lambda b, s: (b * steps + s, 0)),
        out_shape=jax.ShapeDtypeStruct((batch * seq, ATTN_DIM), jnp.bfloat16),
        scratch_shapes=[pltpu.VMEM((META_PAD, PANEL), jnp.bfloat16),
                        pltpu.VMEM((META_PAD, PANEL), jnp.bfloat16),
                        pltpu.VMEM((ATTN_HEADS, N_PAIR, GRID_W, LANES), jnp.float32)],
        compiler_params=pltpu.CompilerParams(
            dimension_semantics=("arbitrary", "arbitrary"),
            vmem_limit_bytes=V7X_VMEM_LIMIT_BYTES),
        name="attention",
    )(proj, proj, proj, proj_meta, proj_meta, rpb_rows)


def _mix_kernel(b_ref, c_ref, u_ref, cp_ref, up_ref, cn_ref, un_ref, cm_ref, um_ref,
                ya_ref, x_ref, cw_ref, cb_ref, wo_ref, o_ref, *, blocks_per_seq):
    i = pl.program_id(0)
    tm = b_ref.shape[0]
    f32 = jnp.float32
    v = c_ref[...].astype(f32) * u_ref[...].astype(f32)
    last = BF16_SUBLANES - 1
    v_before = cp_ref[last:, :].astype(f32) * up_ref[last:, :].astype(f32)
    v_meta = cm_ref[N_META - 1:, :].astype(f32) * um_ref[N_META - 1:, :].astype(f32)
    v_after = cn_ref[0:1, :].astype(f32) * un_ref[0:1, :].astype(f32)
    seq_first = (i % blocks_per_seq) == 0
    seq_last = (i % blocks_per_seq) == blocks_per_seq - 1
    v_before = jnp.where(seq_first, v_meta, v_before)
    v_after = jnp.where(seq_last, 0.0, v_after)
    t = lax.broadcasted_iota(jnp.int32, (tm, 1), 0)
    v_prev = jnp.where(t == 0, v_before, pltpu.roll(v, 1, axis=0))
    v_next = jnp.where(t == tm - 1, v_after, pltpu.roll(v, tm - 1, axis=0))
    conv = (v_prev * cw_ref[0:1, :] + v * cw_ref[1:2, :] + v_next * cw_ref[2:3, :]
            + cb_ref[...])
    y = b_ref[...].astype(f32) * conv
    yc = (y * _inv_rms(y)).astype(jnp.bfloat16)
    acc = jnp.dot(yc, wo_ref[0:CONV_DIM, :], preferred_element_type=f32)
    acc = acc + jnp.dot(ya_ref[...], wo_ref[CONV_DIM:, :], preferred_element_type=f32)
    o_ref[...] = x_ref[...] + acc


def _mix(proj, proj_meta, y_attn, x2d, conv_w, conv_b, w_out_bf16, seq, tm):
    rows = x2d.shape[0]
    nblk = rows // tm
    hb = tm // BF16_SUBLANES
    n_hb = rows // BF16_SUBLANES
    panel = lambda p: pl.BlockSpec((None, tm, PANEL), lambda i, p=p: (p, i, 0))
    prev = lambda p: pl.BlockSpec((None, BF16_SUBLANES, PANEL),
                                  lambda i, p=p: (p, jnp.maximum(i * hb - 1, 0), 0))
    nxt = lambda p: pl.BlockSpec((None, BF16_SUBLANES, PANEL),
                                 lambda i, p=p: (p, jnp.minimum((i + 1) * hb, n_hb - 1), 0))
    meta = lambda p: pl.BlockSpec((None, N_META, PANEL), lambda i, p=p: (p, 0, 0))
    vec = lambda n: pl.BlockSpec((n, CONV_DIM), lambda i: (0, 0))
    return pl.pallas_call(
        functools.partial(_mix_kernel, blocks_per_seq=seq // tm),
        grid=(nblk,),
        in_specs=[
            panel(P_B), panel(P_C), panel(P_U),
            prev(P_C), prev(P_U), nxt(P_C), nxt(P_U), meta(P_C), meta(P_U),
            pl.BlockSpec((tm, ATTN_DIM), lambda i: (i, 0)),
            pl.BlockSpec((tm, D_MODEL), lambda i: (i, 0)),
            vec(3), vec(1),
            pl.BlockSpec((D_MODEL, D_MODEL), lambda i: (0, 0),
                         pipeline_mode=pl.Buffered(1)),
        ],
        out_specs=pl.BlockSpec((tm, D_MODEL), lambda i: (i, 0)),
        out_shape=jax.ShapeDtypeStruct((rows, D_MODEL), jnp.float32),
        compiler_params=pltpu.CompilerParams(
            dimension_semantics=("arbitrary",),
            vmem_limit_bytes=V7X_VMEM_LIMIT_BYTES),
        name="mix",
    )(proj, proj, proj, proj, proj, proj, proj, proj_meta, proj_meta,
      y_attn, x2d, conv_w, conv_b, w_out_bf16)


def _mlp_step(f, nf, h_ref, wu_ref, wd_ref, gf_ref, o_ref, hn_ref, inv_ref, prepare=None):
    @pl.when(f == 0)
    def _():
        h = h_ref[...]
        _split_norm(h, hn_ref, inv_ref, slice(None))
        o_ref[...] = h

    if prepare is not None:
        prepare()
    wu = jnp.concatenate([wu_ref[s] for s in range(wu_ref.shape[0])], axis=1)
    a = jnp.dot(hn_ref[...], wu, preferred_element_type=jnp.float32)
    a = _scale_rows(a, inv_ref[...])
    a = jnp.square(jnp.maximum(a, 0.0)).astype(jnp.bfloat16)
    o_ref[...] += jnp.dot(a, wd_ref[...], preferred_element_type=jnp.float32)

    @pl.when(f == nf - 1)
    def _():
        o_ref[...] = _rms(o_ref[...], gf_ref[...])


def _mlp_first_kernel(h_ref, g2_ref, wu_ref, wd_ref, gf_ref, o_ref, wub_ref, wdb_ref,
                      hn_ref, inv_ref):
    def cast_tiles():
        slab_w = wub_ref.shape[2]
        g2 = g2_ref[...]
        for s in range(wub_ref.shape[0]):
            wub_ref[s] = (wu_ref[:, s * slab_w:(s + 1) * slab_w] * g2).astype(jnp.bfloat16)
        wdb_ref[...] = wd_ref[...].astype(jnp.bfloat16)

    _mlp_step(pl.program_id(0), pl.num_programs(0),
              h_ref, wub_ref, wdb_ref, gf_ref, o_ref, hn_ref, inv_ref, prepare=cast_tiles)


def _mlp_rest_kernel(h_ref, wu_ref, wd_ref, gf_ref, o_ref, hn_ref, inv_ref):
    _mlp_step(pl.program_id(1), pl.num_programs(1),
              h_ref, wu_ref, wd_ref, gf_ref, o_ref, hn_ref, inv_ref)


UP_SLAB = 256


def _mlp(h1, g2_col, w_up_f32, w_down_f32, gf, tm, tf, tf_rest):
    rows = h1.shape[0]
    n_slab = tf // UP_SLAB
    params = lambda sem: pltpu.CompilerParams(
        dimension_semantics=sem, vmem_limit_bytes=V7X_VMEM_LIMIT_BYTES)
    once = pl.Buffered(1)
    out, w_up_b, w_down_b = pl.pallas_call(
        _mlp_first_kernel,
        grid=(D_FF // tf,),
        in_specs=[
            pl.BlockSpec((tm, D_MODEL), lambda f: (0, 0), pipeline_mode=once),
            pl.BlockSpec((D_MODEL, 1), lambda f: (0, 0)),
            pl.BlockSpec((D_MODEL, tf), lambda f: (0, f)),
            pl.BlockSpec((tf, D_MODEL), lambda f: (f, 0)),
            pl.BlockSpec((1, D_MODEL), lambda f: (0, 0)),
        ],
        out_specs=[
            pl.BlockSpec((tm, D_MODEL), lambda f: (0, 0), pipeline_mode=once),
            pl.BlockSpec((n_slab, D_MODEL, UP_SLAB), lambda f: (f, 0, 0)),
            pl.BlockSpec((tf, D_MODEL), lambda f: (f, 0)),
        ],
        out_shape=[
            jax.ShapeDtypeStruct((rows, D_MODEL), jnp.float32),
            jax.ShapeDtypeStruct((D_FF // UP_SLAB, D_MODEL, UP_SLAB), jnp.bfloat16),
            jax.ShapeDtypeStruct((D_FF, D_MODEL), jnp.bfloat16),
        ],
        scratch_shapes=[pltpu.VMEM((tm, D_MODEL), jnp.bfloat16),
                        pltpu.VMEM((tm, LANES), jnp.float32)],
        input_output_aliases={0: 0},
        compiler_params=params(("arbitrary",)),
        name="mlp_first",
    )(h1, g2_col, w_up_f32, w_down_f32, gf)
    return pl.pallas_call(
        _mlp_rest_kernel,
        grid=(rows // tm - 1, D_FF // tf_rest),
        in_specs=[
            pl.BlockSpec((tm, D_MODEL), lambda i, f: (i + 1, 0)),
            pl.BlockSpec((tf_rest // UP_SLAB, D_MODEL, UP_SLAB), lambda i, f: (f, 0, 0)),
            pl.BlockSpec((tf_rest, D_MODEL), lambda i, f: (f, 0)),
            pl.BlockSpec((1, D_MODEL), lambda i, f: (0, 0)),
        ],
        out_specs=pl.BlockSpec((tm, D_MODEL), lambda i, f: (i + 1, 0)),
        out_shape=jax.ShapeDtypeStruct((rows, D_MODEL), jnp.float32),
        scratch_shapes=[pltpu.VMEM((tm, D_MODEL), jnp.bfloat16),
                        pltpu.VMEM((tm, LANES), jnp.float32)],
        input_output_aliases={0: 0},
        compiler_params=params(("arbitrary", "arbitrary")),
        name="mlp_rest",
    )(out, w_up_b, w_down_b, gf)


def kernel(x, meta_tokens, norm1_g, w_in, conv_w, conv_b, conv_norm_g, attn_rpb,
           attn_norm_g, w_out, norm2_g, w_up, w_down, final_norm_g):
    batch, seq, _ = x.shape
    x2d = x.reshape(batch * seq, D_MODEL)
    g_mix = jnp.concatenate([conv_norm_g[0], attn_norm_g[0]]).reshape(D_MODEL, 1)
    proj, proj_meta, w_out_b = _in_proj(x2d, meta_tokens, norm1_g.reshape(D_MODEL, 1), w_in[0],
                                        w_out[0], g_mix, tm=1024)
    y_attn = _attention(proj, proj_meta, attn_rpb[0], batch, seq)
    h1 = _mix(proj, proj_meta, y_attn, x2d, conv_w[0], conv_b, w_out_b, seq, tm=512)
    out = _mlp(h1, norm2_g.reshape(D_MODEL, 1), w_up[0], w_down[0],
               final_norm_g.reshape(1, D_MODEL), tm=1024, tf=512, tf_rest=1024)
    return out.reshape(batch, seq, D_MODEL)
```

```python
import functools

import jax
import jax.numpy as jnp
from jax import lax
from jax.experimental import pallas as pl
from jax.experimental.pallas import tpu as pltpu

D_MODEL = 2048
N_META = 16
CONV_DIM = D_MODEL // 2
ATTN_HEADS = 16
HEAD_DIM = 64
ATTN_DIM = ATTN_HEADS * HEAD_DIM
PROJ_DIM = 3 * CONV_DIM + 3 * ATTN_DIM
D_FF = 4 * D_MODEL
GRID_W = 64
WIN_ROWS = 8
WIN_COLS = 16
RMS_EPS = 1e-6
NEG_INF = -1e30

PANEL = 1024
P_B, P_C, P_U, P_Q, P_K, P_V = range(6)

V7X_VMEM_LIMIT_BYTES = 60 * 1024 * 1024
BF16_SUBLANES = 16
LANES = 128

HEADS_PER_GROUP = 4
GROUP_W = HEADS_PER_GROUP * HEAD_DIM
N_GROUPS = ATTN_HEADS // HEADS_PER_GROUP
ROWS_PER_STEP = 8
WIN_KEYS = WIN_ROWS * GRID_W
META_PAD = LANES
ALL_KEYS = WIN_KEYS + META_PAD
N_DR = 2 * WIN_ROWS - 1
N_DC = 2 * WIN_COLS - 1
N_PAIR = N_DR - 1
ROW_UNROLL = 8


def _inv_rms(xf):
    return lax.rsqrt(jnp.mean(xf * xf, axis=-1, keepdims=True) + RMS_EPS)


def _rms(xf, g):
    return xf * _inv_rms(xf) * g


def _split_norm(xf, hn_ref, inv_ref, rows):
    hn_ref[rows, :] = xf.astype(hn_ref.dtype)
    inv_ref[rows, :] = jnp.broadcast_to(_inv_rms(xf), (xf.shape[0], LANES))


def _gain_column(g_row, col_ref, rows):
    col_ref[rows, :] = jnp.transpose(jnp.broadcast_to(g_row, (LANES, g_row.shape[1])))


def _scale_rows(r, row_scale):
    return jnp.concatenate([r[:, k:k + LANES] * row_scale
                            for k in range(0, r.shape[1], LANES)], axis=1)


LOG2E = 1.4426950408889634
Q_SCALE = HEAD_DIM ** -0.5 * LOG2E


def _panel_scale(panel):
    return jnp.where(panel == P_Q, Q_SCALE, 1.0).astype(jnp.float32)


def _in_proj_first_kernel(x_ref, meta_ref, g_ref, w_ref, o_ref, om_ref, wb_ref,
                          hn_ref, inv_ref, gcol_ref):
    tm = x_ref.shape[0]

    @pl.when(pl.program_id(0) == 0)
    def _():
        _split_norm(x_ref[...], hn_ref, inv_ref, slice(0, tm))
        _split_norm(meta_ref[...], hn_ref, inv_ref, slice(tm, tm + N_META))
        _gain_column(g_ref[...], gcol_ref, slice(None))

    w = _scale_rows(w_ref[...], gcol_ref[...]).astype(jnp.bfloat16)
    wb_ref[...] = w
    r = jnp.dot(hn_ref[...], w, preferred_element_type=jnp.float32)
    scale = inv_ref[...] * _panel_scale(pl.program_id(0))
    o_ref[...] = _scale_rows(r[0:tm], scale[0:tm]).astype(o_ref.dtype)
    om_ref[...] = _scale_rows(r[tm:], scale[tm:]).astype(om_ref.dtype)


def _in_proj_rest_kernel(x_ref, w_ref, first_ref, wo_ref, gc_ref, ga_ref, o_ref, wob_ref,
                         hn_ref, inv_ref, gcol_ref):
    i = pl.program_id(0)

    @pl.when(i == 0)
    def _():
        o_ref[...] = first_ref[...]

        @pl.when(pl.program_id(1) == 0)
        def _():
            _gain_column(gc_ref[...], gcol_ref, slice(0, CONV_DIM))
            _gain_column(ga_ref[...], gcol_ref, slice(CONV_DIM, D_MODEL))

    @pl.when(i > 0)
    def _():
        @pl.when(pl.program_id(1) == 0)
        def _():
            _split_norm(x_ref[...], hn_ref, inv_ref, slice(None))

        for p in range(w_ref.shape[0]):
            r = jnp.dot(hn_ref[...], w_ref[p], preferred_element_type=jnp.float32)
            panel = pl.program_id(1) * w_ref.shape[0] + p
            o_ref[p] = _scale_rows(r, inv_ref[...] * _panel_scale(panel)).astype(o_ref.dtype)

    rows = wo_ref.shape[0]
    slab = jnp.minimum(i * pl.num_programs(1) + pl.program_id(1), CAST_STEPS - 1)
    gains = gcol_ref[pl.ds(pl.multiple_of(slab * rows, rows), rows), :]
    wob_ref[...] = _scale_rows(wo_ref[...], gains).astype(jnp.bfloat16)


CAST_STEPS = 16
PANELS_PER_STEP = 2


def _in_proj(x2d, meta, g, w_f32, w_out, g_conv, g_attn, tm):
    rows = x2d.shape[0]
    nj = PROJ_DIM // PANEL
    ni = rows // tm
    pp = PANELS_PER_STEP
    njr = nj // pp
    assert ni * njr >= CAST_STEPS
    slab = lambda i, j: jnp.minimum(i * njr + j, CAST_STEPS - 1)
    out_spec = pl.BlockSpec((D_MODEL // CAST_STEPS, D_MODEL), lambda i, j: (slab(i, j), 0))
    gain_spec = pl.BlockSpec((1, CONV_DIM), lambda i, j: (0, 0))
    params = lambda sem: pltpu.CompilerParams(
        dimension_semantics=sem, vmem_limit_bytes=V7X_VMEM_LIMIT_BYTES)
    proj0, proj_meta, w_bf16 = pl.pallas_call(
        _in_proj_first_kernel,
        grid=(nj,),
        in_specs=[
            pl.BlockSpec((tm, D_MODEL), lambda j: (0, 0)),
            pl.BlockSpec((N_META, D_MODEL), lambda j: (0, 0)),
            pl.BlockSpec((1, D_MODEL), lambda j: (0, 0)),
            pl.BlockSpec((D_MODEL, PANEL), lambda j: (0, j)),
        ],
        out_specs=[
            pl.BlockSpec((None, tm, PANEL), lambda j: (j, 0, 0)),
            pl.BlockSpec((None, N_META, PANEL), lambda j: (j, 0, 0)),
            pl.BlockSpec((None, D_MODEL, PANEL), lambda j: (j, 0, 0)),
        ],
        out_shape=[
            jax.ShapeDtypeStruct((nj, tm, PANEL), jnp.bfloat16),
            jax.ShapeDtypeStruct((nj, N_META, PANEL), jnp.bfloat16),
            jax.ShapeDtypeStruct((nj, D_MODEL, PANEL), jnp.bfloat16),
        ],
        scratch_shapes=[pltpu.VMEM((tm + N_META, D_MODEL), jnp.bfloat16),
                        pltpu.VMEM((tm + N_META, LANES), jnp.float32),
                        pltpu.VMEM((D_MODEL, LANES), jnp.float32)],
        compiler_params=params(("arbitrary",)),
        name="in_proj_first",
    )(x2d, meta, g, w_f32)
    proj, w_out_b = pl.pallas_call(
        _in_proj_rest_kernel,
        grid=(ni, njr),
        in_specs=[
            pl.BlockSpec((tm, D_MODEL), lambda i, j: (jnp.maximum(i, 1), 0)),
            pl.BlockSpec((pp, D_MODEL, PANEL), lambda i, j: (jnp.where(i == 0, 0, j), 0, 0)),
            pl.BlockSpec((pp, tm, PANEL), lambda i, j: (jnp.where(i == 0, j, njr - 1), 0, 0),
                         pipeline_mode=pl.Buffered(1)),
            out_spec, gain_spec, gain_spec,
        ],
        out_specs=[pl.BlockSpec((pp, tm, PANEL), lambda i, j: (j, i, 0)), out_spec],
        out_shape=[jax.ShapeDtypeStruct((nj, rows, PANEL), jnp.bfloat16),
                   jax.ShapeDtypeStruct(w_out.shape, jnp.bfloat16)],
        scratch_shapes=[pltpu.VMEM((tm, D_MODEL), jnp.bfloat16),
                        pltpu.VMEM((tm, LANES), jnp.float32),
                        pltpu.VMEM((D_MODEL, LANES), jnp.float32)],
        compiler_params=params(("arbitrary", "arbitrary")),
        name="in_proj_rest",
    )(x2d, w_bf16, proj0, w_out, g_conv, g_attn)
    return proj, proj_meta, w_out_b


def _bias_tiles(rpb_ref, bias_ref):
    shape = (GRID_W, LANES)
    qc = lax.broadcasted_iota(jnp.int32, shape, 0)
    lane = lax.broadcasted_iota(jnp.int32, shape, 1)
    kc = lane % GRID_W
    c0 = jnp.clip(qc - WIN_COLS // 2, 0, GRID_W - WIN_COLS)
    valid = (kc >= c0) & (kc < c0 + WIN_COLS)
    low_half = lane < GRID_W

    def head_body(h, carry):
        def toeplitz(dr, lane0):
            row = jnp.broadcast_to(rpb_ref[h, dr:dr + 1, :], shape)
            return pltpu.roll(row, (lane0 - (WIN_COLS - 1)) % LANES, 1, stride=1, stride_axis=0)

        for dr0 in range(N_PAIR):
            tile = jnp.where(low_half, toeplitz(dr0, 0), toeplitz(dr0 + 1, GRID_W))
            bias_ref[h, dr0] = jnp.where(valid, tile * LOG2E, NEG_INF)
        return carry

    lax.fori_loop(0, ATTN_HEADS, head_body, 0)


def _attn_kernel(q_ref, k_ref, v_ref, km_ref, vm_ref, rpb_ref, o_ref,
                 kmp_ref, vmp_ref, bias_ref):
    grp = pl.program_id(1)

    @pl.when(jnp.logical_and(pl.program_id(0) == 0, grp == 0))
    def _():
        _bias_tiles(rpb_ref, bias_ref)

    kmp_ref[...] = jnp.zeros_like(kmp_ref)
    vmp_ref[...] = jnp.zeros_like(vmp_ref)
    kmp_ref[0:N_META, :] = km_ref[...]
    vmp_ref[0:N_META, :] = vm_ref[...]

    row_head = lax.broadcasted_iota(jnp.int32, (GROUP_W, GROUP_W), 0) // HEAD_DIM
    col_head = lax.broadcasted_iota(jnp.int32, (GROUP_W, GROUP_W), 1) // HEAD_DIM
    diag = row_head == col_head
    lane = lax.broadcasted_iota(jnp.int32, (GRID_W, LANES), 1)
    low_half = lane < HEAD_DIM
    meta_mask = jnp.where(lane < N_META, 0.0, NEG_INF)
    nt = (((1,), (1,)), ((), ()))

    def row_body(rl, carry):
        r = grp * ROWS_PER_STEP + rl
        rs = jnp.clip(r - WIN_ROWS // 2, 0, GRID_W - WIN_ROWS)
        case = r - rs
        q0 = pl.multiple_of(rl * GRID_W, GRID_W)
        k0 = pl.multiple_of(rs * GRID_W, GRID_W)
        outs = []
        for hg in range(N_GROUPS):
            lanes = slice(hg * GROUP_W, (hg + 1) * GROUP_W)
            qg = q_ref[pl.ds(q0, GRID_W), lanes]
            qbd = jnp.where(diag, jnp.concatenate([qg] * HEADS_PER_GROUP, axis=0),
                            jnp.zeros((), jnp.bfloat16))
            kx = jnp.concatenate(
                [k_ref[pl.ds(k0, WIN_KEYS), lanes], kmp_ref[:, lanes]], axis=0)
            vx = jnp.concatenate(
                [v_ref[pl.ds(k0, WIN_KEYS), lanes], vmp_ref[:, lanes]], axis=0)
            s = lax.dot_general(qbd, kx, nt, preferred_element_type=jnp.float32)
            bias = jnp.concatenate([
                jnp.concatenate(
                    [bias_ref[hg * HEADS_PER_GROUP + h, 2 * p + (WIN_ROWS - 1) - case]
                     for p in range(WIN_ROWS // 2)] + [meta_mask], axis=1)
                for h in range(HEADS_PER_GROUP)], axis=0)
            s = s + bias
            p = jnp.exp2(s - jnp.max(s, axis=-1, keepdims=True))
            inv_l = 1.0 / jnp.sum(p, axis=-1, keepdims=True)
            o_all = jnp.dot(p.astype(jnp.bfloat16), vx,
                            preferred_element_type=jnp.float32)
            tiles = []
            for t in range(GROUP_W // LANES):
                ra = slice(2 * t * GRID_W, (2 * t + 1) * GRID_W)
                rb = slice((2 * t + 1) * GRID_W, (2 * t + 2) * GRID_W)
                lt = slice(t * LANES, (t + 1) * LANES)
                tiles.append(jnp.where(low_half, o_all[ra, lt] * inv_l[ra],
                                       o_all[rb, lt] * inv_l[rb]))
            outs.append(jnp.concatenate(tiles, axis=1))
        ssq = sum(jnp.sum(o * o, axis=-1, keepdims=True) for o in outs)
        inv = lax.rsqrt(ssq / ATTN_DIM + RMS_EPS)
        for hg in range(N_GROUPS):
            lanes = slice(hg * GROUP_W, (hg + 1) * GROUP_W)
            o_ref[pl.ds(q0, GRID_W), lanes] = (
                outs[hg] * inv).astype(o_ref.dtype)
        return carry

    lax.fori_loop(0, ROWS_PER_STEP, row_body, 0, unroll=ROW_UNROLL)


def _attention(proj, proj_meta, rpb, batch, seq):
    rows = seq // GRID_W
    steps = rows // ROWS_PER_STEP
    tq = ROWS_PER_STEP * GRID_W
    single = pl.Buffered(1)
    rpb_rows = jnp.pad(rpb, ((0, 0), (0, 1), (0, LANES - N_DC)))
    return pl.pallas_call(
        _attn_kernel,
        grid=(batch, steps),
        in_specs=[
            pl.BlockSpec((None, tq, PANEL), lambda b, s: (P_Q, b * steps + s, 0)),
            pl.BlockSpec((None, seq, PANEL), lambda b, s: (P_K, b, 0), pipeline_mode=single),
            pl.BlockSpec((None, seq, PANEL), lambda b, s: (P_V, b, 0), pipeline_mode=single),
            pl.BlockSpec((None, N_META, PANEL), lambda b, s: (P_K, 0, 0)),
            pl.BlockSpec((None, N_META, PANEL), lambda b, s: (P_V, 0, 0)),
            pl.BlockSpec(rpb_rows.shape, lambda b, s: (0, 0, 0)),
        ],
        out_specs=pl.BlockSpec((tq, ATTN_DIM), lambda b, s: (b * steps + s, 0)),
        out_shape=jax.ShapeDtypeStruct((batch * seq, ATTN_DIM), jnp.bfloat16),
        scratch_shapes=[pltpu.VMEM((META_PAD, PANEL), jnp.bfloat16),
                        pltpu.VMEM((META_PAD, PANEL), jnp.bfloat16),
                        pltpu.VMEM((ATTN_HEADS, N_PAIR, GRID_W, LANES), jnp.float32)],
        compiler_params=pltpu.CompilerParams(
            dimension_semantics=("arbitrary", "arbitrary"),
            vmem_limit_bytes=V7X_VMEM_LIMIT_BYTES),
        name="attention",
    )(proj, proj, proj, proj_meta, proj_meta, rpb_rows)


def _mix_kernel(b_ref, c_ref, u_ref, cp_ref, up_ref, cn_ref, un_ref, cm_ref, um_ref,
                ya_ref, x_ref, cw_ref, cb_ref, wo_ref, o_ref, *, blocks_per_seq):
    i = pl.program_id(0)
    tm = b_ref.shape[0]
    f32 = jnp.float32
    v = c_ref[...].astype(f32) * u_ref[...].astype(f32)
    last = BF16_SUBLANES - 1
    v_before = cp_ref[last:, :].astype(f32) * up_ref[last:, :].astype(f32)
    v_meta = cm_ref[N_META - 1:, :].astype(f32) * um_ref[N_META - 1:, :].astype(f32)
    v_after = cn_ref[0:1, :].astype(f32) * un_ref[0:1, :].astype(f32)
    seq_first = (i % blocks_per_seq) == 0
    seq_last = (i % blocks_per_seq) == blocks_per_seq - 1
    v_before = jnp.where(seq_first, v_meta, v_before)
    v_after = jnp.where(seq_last, 0.0, v_after)
    t = lax.broadcasted_iota(jnp.int32, (tm, 1), 0)
    v_prev = jnp.where(t == 0, v_before, pltpu.roll(v, 1, axis=0))
    v_next = jnp.where(t == tm - 1, v_after, pltpu.roll(v, tm - 1, axis=0))
    conv = (v_prev * cw_ref[0:1, :] + v * cw_ref[1:2, :] + v_next * cw_ref[2:3, :]
            + cb_ref[...])
    y = b_ref[...].astype(f32) * conv
    yc = (y * _inv_rms(y)).astype(jnp.bfloat16)
    acc = jnp.dot(yc, wo_ref[0:CONV_DIM, :], preferred_element_type=f32)
    acc = acc + jnp.dot(ya_ref[...], wo_ref[CONV_DIM:, :], preferred_element_type=f32)
    o_ref[...] = x_ref[...] + acc


def _mix(proj, proj_meta, y_attn, x2d, conv_w, conv_b, w_out_bf16, seq, tm):
    rows = x2d.shape[0]
    nblk = rows // tm
    hb = tm // BF16_SUBLANES
    n_hb = rows // BF16_SUBLANES
    panel = lambda p: pl.BlockSpec((None, tm, PANEL), lambda i, p=p: (p, i, 0))
    prev = lambda p: pl.BlockSpec((None, BF16_SUBLANES, PANEL),
                                  lambda i, p=p: (p, jnp.maximum(i * hb - 1, 0), 0))
    nxt = lambda p: pl.BlockSpec((None, BF16_SUBLANES, PANEL),
                                 lambda i, p=p: (p, jnp.minimum((i + 1) * hb, n_hb - 1), 0))
    meta = lambda p: pl.BlockSpec((None, N_META, PANEL), lambda i, p=p: (p, 0, 0))
    vec = lambda n: pl.BlockSpec((n, CONV_DIM), lambda i: (0, 0))
    return pl.pallas_call(
        functools.partial(_mix_kernel, blocks_per_seq=seq // tm),
        grid=(nblk,),
        in_specs=[
            panel(P_B), panel(P_C), panel(P_U),
            prev(P_C), prev(P_U), nxt(P_C), nxt(P_U), meta(P_C), meta(P_U),
            pl.BlockSpec((tm, ATTN_DIM), lambda i: (i, 0)),
            pl.BlockSpec((tm, D_MODEL), lambda i: (i, 0)),
            vec(3), vec(1),
            pl.BlockSpec((D_MODEL, D_MODEL), lambda i: (0, 0),
                         pipeline_mode=pl.Buffered(1)),
        ],
        out_specs=pl.BlockSpec((tm, D_MODEL), lambda i: (i, 0)),
        out_shape=jax.ShapeDtypeStruct((rows, D_MODEL), jnp.float32),
        compiler_params=pltpu.CompilerParams(
            dimension_semantics=("arbitrary",),
            vmem_limit_bytes=V7X_VMEM_LIMIT_BYTES),
        name="mix",
    )(proj, proj, proj, proj, proj, proj, proj, proj_meta, proj_meta,
      y_attn, x2d, conv_w, conv_b, w_out_bf16)


def _mlp_step(f, nf, h_ref, wu_ref, wd_ref, gf_ref, o_ref, hn_ref, inv_ref, prepare=None):
    @pl.when(f == 0)
    def _():
        h = h_ref[...]
        _split_norm(h, hn_ref, inv_ref, slice(None))
        o_ref[...] = h

    if prepare is not None:
        prepare()
    wu = jnp.concatenate([wu_ref[s] for s in range(wu_ref.shape[0])], axis=1)
    a = jnp.dot(hn_ref[...], wu, preferred_element_type=jnp.float32)
    a = _scale_rows(a, inv_ref[...])
    a = jnp.square(jnp.maximum(a, 0.0)).astype(jnp.bfloat16)
    o_ref[...] += jnp.dot(a, wd_ref[...], preferred_element_type=jnp.float32)

    @pl.when(f == nf - 1)
    def _():
        o_ref[...] = _rms(o_ref[...], gf_ref[...])


def _mlp_first_kernel(h_ref, g2_ref, wu_ref, wd_ref, gf_ref, o_ref, wub_ref, wdb_ref,
                      hn_ref, inv_ref, gcol_ref):
    @pl.when(pl.program_id(0) == 0)
    def _():
        _gain_column(g2_ref[...], gcol_ref, slice(None))

    def cast_tiles():
        slab_w = wub_ref.shape[2]
        g2 = gcol_ref[...]
        for s in range(wub_ref.shape[0]):
            wub_ref[s] = _scale_rows(wu_ref[:, s * slab_w:(s + 1) * slab_w],
                                     g2).astype(jnp.bfloat16)
        wdb_ref[...] = wd_ref[...].astype(jnp.bfloat16)

    _mlp_step(pl.program_id(0), pl.num_programs(0),
              h_ref, wub_ref, wdb_ref, gf_ref, o_ref, hn_ref, inv_ref, prepare=cast_tiles)


def _mlp_rest_kernel(h_ref, wu_ref, wd_ref, gf_ref, o_ref, hn_ref, inv_ref):
    _mlp_step(pl.program_id(1), pl.num_programs(1),
              h_ref, wu_ref, wd_ref, gf_ref, o_ref, hn_ref, inv_ref)


UP_SLAB = 256


def _mlp(h1, g2, w_up_f32, w_down_f32, gf, tm, tf, tf_rest):
    rows = h1.shape[0]
    n_slab = tf // UP_SLAB
    params = lambda sem: pltpu.CompilerParams(
        dimension_semantics=sem, vmem_limit_bytes=V7X_VMEM_LIMIT_BYTES)
    once = pl.Buffered(1)
    out, w_up_b, w_down_b = pl.pallas_call(
        _mlp_first_kernel,
        grid=(D_FF // tf,),
        in_specs=[
            pl.BlockSpec((tm, D_MODEL), lambda f: (0, 0), pipeline_mode=once),
            pl.BlockSpec((1, D_MODEL), lambda f: (0, 0)),
            pl.BlockSpec((D_MODEL, tf), lambda f: (0, f)),
            pl.BlockSpec((tf, D_MODEL), lambda f: (f, 0)),
            pl.BlockSpec((1, D_MODEL), lambda f: (0, 0)),
        ],
        out_specs=[
            pl.BlockSpec((tm, D_MODEL), lambda f: (0, 0), pipeline_mode=once),
            pl.BlockSpec((n_slab, D_MODEL, UP_SLAB), lambda f: (f, 0, 0)),
            pl.BlockSpec((tf, D_MODEL), lambda f: (f, 0)),
        ],
        out_shape=[
            jax.ShapeDtypeStruct((rows, D_MODEL), jnp.float32),
            jax.ShapeDtypeStruct((D_FF // UP_SLAB, D_MODEL, UP_SLAB), jnp.bfloat16),
            jax.ShapeDtypeStruct((D_FF, D_MODEL), jnp.bfloat16),
        ],
        scratch_shapes=[pltpu.VMEM((tm, D_MODEL), jnp.bfloat16),
                        pltpu.VMEM((tm, LANES), jnp.float32),
                        pltpu.VMEM((D_MODEL, LANES), jnp.float32)],
        input_output_aliases={0: 0},
        compiler_params=params(("arbitrary",)),
        name="mlp_first",
    )(h1, g2, w_up_f32, w_down_f32, gf)
    return pl.pallas_call(
        _mlp_rest_kernel,
        grid=(rows // tm - 1, D_FF // tf_rest),
        in_specs=[
            pl.BlockSpec((tm, D_MODEL), lambda i, f: (i + 1, 0)),
            pl.BlockSpec((tf_rest // UP_SLAB, D_MODEL, UP_SLAB), lambda i, f: (f, 0, 0)),
            pl.BlockSpec((tf_rest, D_MODEL), lambda i, f: (f, 0)),
            pl.BlockSpec((1, D_MODEL), lambda i, f: (0, 0)),
        ],
        out_specs=pl.BlockSpec((tm, D_MODEL), lambda i, f: (i + 1, 0)),
        out_shape=jax.ShapeDtypeStruct((rows, D_MODEL), jnp.float32),
        scratch_shapes=[pltpu.VMEM((tm, D_MODEL), jnp.bfloat16),
                        pltpu.VMEM((tm, LANES), jnp.float32)],
        input_output_aliases={0: 0},
        compiler_params=params(("arbitrary", "arbitrary")),
        name="mlp_rest",
    )(out, w_up_b, w_down_b, gf)


def kernel(x, meta_tokens, norm1_g, w_in, conv_w, conv_b, conv_norm_g, attn_rpb,
           attn_norm_g, w_out, norm2_g, w_up, w_down, final_norm_g):
    batch, seq, _ = x.shape
    x2d = x.reshape(batch * seq, D_MODEL)
    proj, proj_meta, w_out_b = _in_proj(x2d, meta_tokens, norm1_g, w_in[0], w_out[0],
                                        conv_norm_g, attn_norm_g, tm=1024)
    y_attn = _attention(proj, proj_meta, attn_rpb[0], batch, seq)
    h1 = _mix(proj, proj_meta, y_attn, x2d, conv_w[0], conv_b, w_out_b, seq, tm=512)
    out = _mlp(h1, norm2_g, w_up[0], w_down[0], final_norm_g.reshape(1, D_MODEL),
               tm=1024, tf=512, tf_rest=1024)
    return out.reshape(batch, seq, D_MODEL)
```
